```python
import jax, jax.numpy as jnp
from jax import lax
import numpy as np

D_MODEL = 4096
BATCH = 2
SEQ = 4096
DEPTH = 1
DEC_BATCH = 128
DEC_SEQ = 1
PAST_LEN = 2048
PAGE_SIZE = 128

FOX_HEADS = 16
FOX_KV_HEADS = 4
FOX_GROUP = FOX_HEADS // FOX_KV_HEADS
FOX_HEAD_DIM = 128
FOX_WIDTH = FOX_HEADS * FOX_HEAD_DIM
FOX_KV_WIDTH = FOX_KV_HEADS * FOX_HEAD_DIM
FOX_Q_BLOCK = 128
FORGET_BIAS = 3.0
GLA_HEADS = 4
GLA_DK = 256
GLA_DV = 512
GLA_QK_WIDTH = GLA_HEADS * GLA_DK
GLA_WIDTH = GLA_HEADS * GLA_DV
GLA_GATE_RANK = 16
GLA_GATE_TEMP = 16.0
GLA_CHUNK = 64
MIX_WIDTH = FOX_WIDTH + GLA_WIDTH
IN_SPLITS = (FOX_WIDTH, FOX_KV_WIDTH, FOX_KV_WIDTH, FOX_HEADS, GLA_QK_WIDTH, GLA_QK_WIDTH, GLA_WIDTH, GLA_GATE_RANK, GLA_WIDTH)
IN_COLS = FOX_WIDTH + 2 * FOX_KV_WIDTH + FOX_HEADS + 2 * GLA_QK_WIDTH + 2 * GLA_WIDTH + GLA_GATE_RANK
N_EXPERTS = 64
TOP_K = 8
N_GROUPS = 8
TOPK_GROUPS = 4
EXPERTS_PER_GROUP = N_EXPERTS // N_GROUPS
EXPERT_FF = 1024
SHARED_FF = 1024
ROUTED_SCALE = 2.5
MOE_BLOCK = 128
NORM_EPS = 1e-6
ADA_SCALE = 0.5

kernel_name = "fox_gla_hybrid_moe_adaln_decode_step"


def rmsnorm(x, g):
    xf = x.astype(jnp.float32)
    y = xf * lax.rsqrt(jnp.mean(xf * xf, axis=-1, keepdims=True) + NORM_EPS)
    return (y * g.astype(jnp.float32)).astype(x.dtype)


def ada_rmsnorm(x, g, shift, scale):
    return rmsnorm(x, g) * (1 + scale) + shift


def modulation(c, w_ada, b_ada):
    mod = jax.nn.silu(c) @ w_ada + b_ada
    return jnp.split(mod[:, None, :], 6, axis=-1)


def mixer_inputs(h, w_in, b_forget, w_gla_gate_up, b_gla_gate):
    B, L, _ = h.shape
    offs = np.cumsum(IN_SPLITS)[:-1].tolist()
    fq, fk, fv, fg, gq, gk, gv, gg, gr = jnp.split(h @ w_in, offs, axis=-1)
    fq = fq.reshape(B, L, FOX_KV_HEADS, FOX_GROUP, FOX_HEAD_DIM)
    fk = fk.reshape(B, L, FOX_KV_HEADS, FOX_HEAD_DIM)
    fv = fv.reshape(B, L, FOX_KV_HEADS, FOX_HEAD_DIM)
    logf = jax.nn.log_sigmoid((fg + b_forget).astype(jnp.float32))
    gq = gq.reshape(B, L, GLA_HEADS, GLA_DK) * (GLA_DK ** -0.5)
    gk = gk.reshape(B, L, GLA_HEADS, GLA_DK)
    gv = gv.reshape(B, L, GLA_HEADS, GLA_DV)
    log_a = (jax.nn.log_sigmoid((gg @ w_gla_gate_up + b_gla_gate).astype(jnp.float32)) / GLA_GATE_TEMP)
    log_a = log_a.reshape(B, L, GLA_HEADS, GLA_DK)
    gr = gr.reshape(B, L, GLA_HEADS, GLA_DV)
    return fq, fk, fv, logf, gq, gk, gv, log_a, gr


def fox_attend(q, cum_q, q_pos, k, v, cum_k, k_pos):
    s = jnp.einsum('bqhgd,bkhd->bhgqk', q, k, preferred_element_type=jnp.float32) * (FOX_HEAD_DIM ** -0.5)
    decay = cum_q.transpose(0, 2, 3, 1)[..., :, None] - cum_k.transpose(0, 2, 3, 1)[..., None, :]
    s = jnp.where(k_pos[None, :] <= q_pos[:, None], s + decay, -jnp.inf)
    p = jax.nn.softmax(s, axis=-1)
    return jnp.einsum('bhgqk,bkhd->bqhgd', p.astype(v.dtype), v)


def fox_prompt(q, k, v, logf):
    B, L = q.shape[:2]
    cum = jnp.cumsum(logf, axis=1).reshape(B, L, FOX_KV_HEADS, FOX_GROUP)
    k_pos = jnp.arange(L)

    def q_block(i):
        start = i * FOX_Q_BLOCK
        qb = lax.dynamic_slice_in_dim(q, start, FOX_Q_BLOCK, axis=1)
        cb = lax.dynamic_slice_in_dim(cum, start, FOX_Q_BLOCK, axis=1)
        return fox_attend(qb, cb, start + jnp.arange(FOX_Q_BLOCK), k, v, cum, k_pos)

    o = lax.map(q_block, jnp.arange(L // FOX_Q_BLOCK))
    return jnp.moveaxis(o, 0, 1).reshape(B, L, FOX_WIDTH)


def fox_sample(q, k_new, v_new, logf_new, cache_k, cache_v, cache_logf, page_table, layer):
    DB, T = q.shape[:2]
    k_past = cache_k[layer, page_table].reshape(DB, -1, FOX_KV_HEADS, FOX_HEAD_DIM)
    v_past = cache_v[layer, page_table].reshape(DB, -1, FOX_KV_HEADS, FOX_HEAD_DIM)
    lf_past = cache_logf[layer, page_table].reshape(DB, -1, FOX_HEADS)
    P = k_past.shape[1]
    k_all = jnp.concatenate([k_past.astype(k_new.dtype), k_new], axis=1)
    v_all = jnp.concatenate([v_past.astype(v_new.dtype), v_new], axis=1)
    lf_all = jnp.concatenate([lf_past.astype(jnp.float32), logf_new], axis=1)
    cum = jnp.cumsum(lf_all, axis=1).reshape(DB, P + T, FOX_KV_HEADS, FOX_GROUP)
    o = fox_attend(q, cum[:, P:], P + jnp.arange(T), k_all, v_all, cum, jnp.arange(P + T))
    return o.reshape(DB, T, FOX_WIDTH)


def gla_chunked(q, k, v, log_a, s0):
    B, L = q.shape[:2]
    c = GLA_CHUNK if L % GLA_CHUNK == 0 else L
    n = L // c
    causal = jnp.tril(jnp.ones((c, c), dtype=bool))

    def to_chunks(t):
        return jnp.moveaxis(t.reshape(B, n, c, *t.shape[2:]), 1, 0)

    def step(S, inp):
        qc, kc, vc, ac = inp
        qf, kf, vf = qc.astype(jnp.float32), kc.astype(jnp.float32), vc.astype(jnp.float32)
        b = jnp.cumsum(ac, axis=1)
        o_inter = jnp.einsum('bchk,bhkv->bchv', qf * jnp.exp(b), S)
        diff = b[:, :, None] - b[:, None, :]
        decay = jnp.exp(jnp.where(causal[None, :, :, None, None], diff, -jnp.inf))
        A = jnp.einsum('bihk,bjhk,bijhk->bhij', qf, kf, decay)
        o_intra = jnp.einsum('bhij,bjhv->bihv', A, vf)
        b_last = b[:, -1]
        S_new = jnp.exp(b_last)[..., None] * S + jnp.einsum('bchk,bchv->bhkv', kf * jnp.exp(b_last[:, None] - b), vf)
        return S_new, o_inter + o_intra

    S_fin, o = lax.scan(step, s0.astype(jnp.float32), (to_chunks(q), to_chunks(k), to_chunks(v), to_chunks(log_a)))
    o = jnp.moveaxis(o, 0, 1).reshape(B, L, GLA_HEADS, GLA_DV)
    return o.astype(v.dtype), S_fin


def merge_heads(o_fox, o_gla, gr, g_gla_out, w_out):
    B, L = o_fox.shape[:2]
    gla = rmsnorm(o_gla, g_gla_out) * jax.nn.silu(gr)
    mixed = jnp.concatenate([o_fox, gla.reshape(B, L, GLA_WIDTH)], axis=-1)
    return mixed @ w_out


def route(t, w_router, b_router):
    T = t.shape[0]
    scores = jax.nn.sigmoid((t @ w_router).astype(jnp.float32))
    sel = scores + b_router.astype(jnp.float32)
    grp_score = lax.top_k(sel.reshape(T, N_GROUPS, EXPERTS_PER_GROUP), 2)[0].sum(-1)
    _, top_g = lax.top_k(grp_score, TOPK_GROUPS)
    gmask = jnp.any(top_g[:, :, None] == jnp.arange(N_GROUPS)[None, None, :], axis=1)
    emask = jnp.repeat(gmask, EXPERTS_PER_GROUP, axis=1)
    _, idx = lax.top_k(jnp.where(emask, sel, -jnp.inf), TOP_K)
    w = jnp.take_along_axis(scores, idx, axis=1)
    w = w / jnp.sum(w, axis=-1, keepdims=True) * ROUTED_SCALE
    return idx, w


def moe_routed(t, idx, wts, w_e_gate, w_e_up, w_e_down, layer):
    T, D = t.shape
    M = T * TOP_K
    flat_e = idx.reshape(-1)
    order = jnp.argsort(flat_e)
    e_sorted = flat_e[order]
    tok_sorted = (order // TOP_K).astype(jnp.int32)
    w_sorted = wts.reshape(-1)[order]
    counts = jnp.bincount(flat_e, length=N_EXPERTS)
    padded = (counts + MOE_BLOCK - 1) // MOE_BLOCK * MOE_BLOCK
    pad_end = jnp.cumsum(padded)
    pad_start = pad_end - padded
    grp_start = jnp.cumsum(counts) - counts
    dest = pad_start[e_sorted] + jnp.arange(M) - grp_start[e_sorted]
    n_blocks = -(-(M + N_EXPERTS * (MOE_BLOCK - 1)) // MOE_BLOCK)
    R = n_blocks * MOE_BLOCK
    slot_tok = jnp.full((R,), T, jnp.int32).at[dest].set(tok_sorted)
    slot_w = jnp.zeros((R,), jnp.float32).at[dest].set(w_sorted)
    block_expert = jnp.minimum(jnp.searchsorted(pad_end, jnp.arange(n_blocks) * MOE_BLOCK, side='right'), N_EXPERTS - 1)
    t_pad = jnp.concatenate([t, jnp.zeros((1, D), t.dtype)], axis=0)

    def expert_block(args):
        toks, e, w = args
        xb = t_pad[toks]
        y = (jax.nn.silu(xb @ w_e_gate[layer, e]) * (xb @ w_e_up[layer, e])) @ w_e_down[layer, e]
        return y * w[:, None].astype(y.dtype)

    yb = lax.map(expert_block, (slot_tok.reshape(n_blocks, MOE_BLOCK), block_expert, slot_w.reshape(n_blocks, MOE_BLOCK)))
    return jax.ops.segment_sum(yb.reshape(R, D), slot_tok, num_segments=T + 1)[:T]


def moe_ffn(h, w_router, b_router, w_e_gate, w_e_up, w_e_down, w_s_gate, w_s_up, w_s_down, layer):
    B, L, D = h.shape
    t = h.reshape(B * L, D)
    idx, wts = route(t, w_router, b_router)
    routed = moe_routed(t, idx, wts, w_e_gate, w_e_up, w_e_down, layer)
    shared = (jax.nn.silu(t @ w_s_gate) * (t @ w_s_up)) @ w_s_down
    return (routed + shared).reshape(B, L, D)


def setup_inputs(seed: int = 0) -> dict:
    key = jax.random.key(seed)
    ks = jax.random.split(key, 32)
    f32 = jnp.float32
    n_pages = PAST_LEN // PAGE_SIZE
    n_used = DEC_BATCH * n_pages
    n_pool = n_used + max(1, n_used // 4)

    def nrm(k, shape, scale=1.0):
        return jax.random.normal(k, shape, f32) * scale

    def gain(k, shape):
        return 1.0 + 0.02 * jax.random.normal(k, shape, f32)

    page_table = jax.random.permutation(ks[0], n_pool)[:n_used].reshape(DEC_BATCH, n_pages).astype(jnp.int32)
    return {
        'x_prompt': nrm(ks[1], (BATCH, SEQ, D_MODEL)),
        'x_sample': nrm(ks[2], (DEC_BATCH, DEC_SEQ, D_MODEL)),
        'c_prompt': nrm(ks[3], (BATCH, D_MODEL)),
        'c_sample': nrm(ks[4], (DEC_BATCH, D_MODEL)),
        'cache_k': nrm(ks[5], (DEPTH, n_pool, PAGE_SIZE, FOX_KV_HEADS, FOX_HEAD_DIM)),
        'cache_v': nrm(ks[6], (DEPTH, n_pool, PAGE_SIZE, FOX_KV_HEADS, FOX_HEAD_DIM)),
        'cache_logf': jax.nn.log_sigmoid(FORGET_BIAS + nrm(ks[7], (DEPTH, n_pool, PAGE_SIZE, FOX_HEADS), 0.5)),
        'state_gla': nrm(ks[8], (DEPTH, DEC_BATCH, GLA_HEADS, GLA_DK, GLA_DV), 0.5),
        'page_table': page_table,
        'w_ada': nrm(ks[9], (DEPTH, D_MODEL, 6 * D_MODEL), ADA_SCALE * D_MODEL ** -0.5),
        'b_ada': nrm(ks[10], (DEPTH, 6 * D_MODEL), 0.02),
        'g_attn_pre': gain(ks[11], (DEPTH, D_MODEL)),
        'g_attn_post': gain(ks[12], (DEPTH, D_MODEL)),
        'g_ffn_pre': gain(ks[13], (DEPTH, D_MODEL)),
        'g_ffn_post': gain(ks[14], (DEPTH, D_MODEL)),
        'w_in': nrm(ks[15], (DEPTH, D_MODEL, IN_COLS), D_MODEL ** -0.5),
        'b_forget': FORGET_BIAS + nrm(ks[16], (DEPTH, FOX_HEADS), 0.5),
        'w_gla_gate_up': nrm(ks[17], (DEPTH, GLA_GATE_RANK, GLA_QK_WIDTH), GLA_GATE_RANK ** -0.5),
        'b_gla_gate': nrm(ks[18], (DEPTH, GLA_QK_WIDTH), 0.1),
        'g_gla_out': gain(ks[19], (DEPTH, GLA_DV)),
        'w_out': nrm(ks[20], (DEPTH, MIX_WIDTH, D_MODEL), MIX_WIDTH ** -0.5),
        'w_router': nrm(ks[21], (DEPTH, D_MODEL, N_EXPERTS), D_MODEL ** -0.5),
        'b_router': nrm(ks[22], (DEPTH, N_EXPERTS), 0.01),
        'w_e_gate': nrm(ks[23], (DEPTH, N_EXPERTS, D_MODEL, EXPERT_FF), D_MODEL ** -0.5),
        'w_e_up': nrm(ks[24], (DEPTH, N_EXPERTS, D_MODEL, EXPERT_FF), D_MODEL ** -0.5),
        'w_e_down': nrm(ks[25], (DEPTH, N_EXPERTS, EXPERT_FF, D_MODEL), EXPERT_FF ** -0.5),
        'w_s_gate': nrm(ks[26], (DEPTH, D_MODEL, SHARED_FF), D_MODEL ** -0.5),
        'w_s_up': nrm(ks[27], (DEPTH, D_MODEL, SHARED_FF), D_MODEL ** -0.5),
        'w_s_down': nrm(ks[28], (DEPTH, SHARED_FF, D_MODEL), SHARED_FF ** -0.5),
    }


def reference(x_prompt, x_sample, c_prompt, c_sample, cache_k, cache_v, cache_logf, state_gla, page_table,
              w_ada, b_ada, g_attn_pre, g_attn_post, g_ffn_pre, g_ffn_post,
              w_in, b_forget, w_gla_gate_up, b_gla_gate, g_gla_out, w_out,
              w_router, b_router, w_e_gate, w_e_up, w_e_down, w_s_gate, w_s_up, w_s_down):
    y_p, y_s = x_prompt, x_sample
    k_p, v_p, f_p, s_p = [], [], [], []
    k_s, v_s, f_s, s_s = [], [], [], []
    for l in range(DEPTH):
        sh_ap, sc_ap, gt_ap, sh_fp, sc_fp, gt_fp = modulation(c_prompt, w_ada[l], b_ada[l])
        sh_as, sc_as, gt_as, sh_fs, sc_fs, gt_fs = modulation(c_sample, w_ada[l], b_ada[l])

        fq, fk, fv, lf, gq, gk, gv, la, gr = mixer_inputs(ada_rmsnorm(y_p, g_attn_pre[l], sh_ap, sc_ap),
                                                          w_in[l], b_forget[l], w_gla_gate_up[l], b_gla_gate[l])
        o_fox = fox_prompt(fq, fk, fv, lf)
        s0 = jnp.zeros((y_p.shape[0], GLA_HEADS, GLA_DK, GLA_DV), jnp.float32)
        o_gla, s_fin = gla_chunked(gq, gk, gv, la, s0)
        y_p = y_p + gt_ap * rmsnorm(merge_heads(o_fox, o_gla, gr, g_gla_out[l], w_out[l]), g_attn_post[l])
        k_p.append(fk); v_p.append(fv); f_p.append(lf); s_p.append(s_fin)

        fq, fk, fv, lf, gq, gk, gv, la, gr = mixer_inputs(ada_rmsnorm(y_s, g_attn_pre[l], sh_as, sc_as),
                                                          w_in[l], b_forget[l], w_gla_gate_up[l], b_gla_gate[l])
        o_fox = fox_sample(fq, fk, fv, lf, cache_k, cache_v, cache_logf, page_table, l)
        o_gla, s_fin = gla_chunked(gq, gk, gv, la, state_gla[l])
        y_s = y_s + gt_as * rmsnorm(merge_heads(o_fox, o_gla, gr, g_gla_out[l], w_out[l]), g_attn_post[l])
        k_s.append(fk); v_s.append(fv); f_s.append(lf); s_s.append(s_fin)

        y_p = y_p + gt_fp * rmsnorm(moe_ffn(ada_rmsnorm(y_p, g_ffn_pre[l], sh_fp, sc_fp), w_router[l], b_router[l],
                                            w_e_gate, w_e_up, w_e_down, w_s_gate[l], w_s_up[l], w_s_down[l], l),
                                    g_ffn_post[l])
        y_s = y_s + gt_fs * rmsnorm(moe_ffn(ada_rmsnorm(y_s, g_ffn_pre[l], sh_fs, sc_fs), w_router[l], b_router[l],
                                            w_e_gate, w_e_up, w_e_down, w_s_gate[l], w_s_up[l], w_s_down[l], l),
                                    g_ffn_post[l])
    return (y_p, y_s, jnp.stack(k_p), jnp.stack(v_p), jnp.stack(f_p), jnp.stack(s_p),
            jnp.stack(k_s), jnp.stack(v_s), jnp.stack(f_s), jnp.stack(s_s))
```

```python
import functools

import numpy as np
import jax
import jax.numpy as jnp
from jax import lax
from jax.experimental import pallas as pl
from jax.experimental.pallas import tpu as pltpu

F32 = jnp.float32
BF16 = jnp.bfloat16
HI = lax.Precision.HIGHEST
NT_DIMS = (((1,), (1,)), ((), ()))
TN_DIMS = (((0,), (0,)), ((), ()))

FOX_HEADS = 16
FOX_KV_HEADS = 4
FOX_GROUP = FOX_HEADS // FOX_KV_HEADS
FOX_HEAD_DIM = 128
FOX_WIDTH = FOX_HEADS * FOX_HEAD_DIM
FOX_KV_WIDTH = FOX_KV_HEADS * FOX_HEAD_DIM
GLA_HEADS = 4
GLA_DK = 256
GLA_DV = 512
GLA_QK_WIDTH = GLA_HEADS * GLA_DK
GLA_WIDTH = GLA_HEADS * GLA_DV
GLA_GATE_RANK = 16
GLA_GATE_TEMP = 16.0
GLA_CHUNK = 64
N_EXPERTS = 64
TOP_K = 8
N_GROUPS = 8
TOPK_GROUPS = 4
EXPERTS_PER_GROUP = N_EXPERTS // N_GROUPS
ROUTED_SCALE = 2.5
NORM_EPS = 1e-6

LANES = 128
SUBLANES = 8
VMEM_LIMIT_BYTES = 56 * 1024 * 1024

COL_FQ = 0
COL_FK = COL_FQ + FOX_WIDTH
COL_FV = COL_FK + FOX_KV_WIDTH
COL_GQ = COL_FV + FOX_KV_WIDTH
COL_GK = COL_GQ + GLA_QK_WIDTH
COL_GV = COL_GK + GLA_QK_WIDTH
COL_GR = COL_GV + GLA_WIDTH
BIG_COLS = COL_GR + GLA_WIDTH
SMALL_COLS = LANES


def _params(*sem):
    return pltpu.CompilerParams(dimension_semantics=sem, vmem_limit_bytes=VMEM_LIMIT_BYTES)


def _sigmoid(x):
    return 1.0 / (1.0 + jnp.exp(-x))


def _silu(x):
    return x * _sigmoid(x)


def _log_sigmoid(x):
    return jnp.minimum(x, 0.0) - jnp.log(1.0 + jnp.exp(-jnp.abs(x)))


def _rms(x, g):
    return x * lax.rsqrt(jnp.mean(x * x, axis=-1, keepdims=True) + NORM_EPS) * g


def _iota(shape, dim):
    return lax.broadcasted_iota(jnp.int32, shape, dim)


def _mod_kernel(c_ref, w_ref, b_ref, o_ref):
    a = _silu(c_ref[...]).astype(BF16)
    o_ref[...] = jnp.dot(a, w_ref[...].astype(BF16), preferred_element_type=F32) + b_ref[...]


def _modulation(c_all, w_ada, b_ada, tn=512):
    mp, d = c_all.shape
    n = w_ada.shape[1]
    return pl.pallas_call(
        _mod_kernel,
        grid=(n // tn,),
        in_specs=[pl.BlockSpec((mp, d), lambda j: (0, 0)),
                  pl.BlockSpec((d, tn), lambda j: (0, j)),
                  pl.BlockSpec((1, tn), lambda j: (0, j))],
        out_specs=pl.BlockSpec((mp, tn), lambda j: (0, j)),
        out_shape=jax.ShapeDtypeStruct((mp, n), F32),
        compiler_params=_params("parallel"),
        name="modulation",
    )(c_all, w_ada, b_ada.reshape(1, n))


def _mod_spec(per_row, which, tm, d, tiles_per_seq):
    if per_row:
        return pl.BlockSpec((None, tm, d), lambda i, *_: (which, i, 0))
    return pl.BlockSpec((None, None, 1, d), lambda i, *_: (i // tiles_per_seq, which, 0, 0))


def _inproj_kernel(x_ref, g_ref, sh_ref, sc_ref, w_ref, ws_ref, bs_ref, o_ref, os_ref, h_ref):
    @pl.when(pl.program_id(1) == 0)
    def _():
        h = _rms(x_ref[...], g_ref[...]) * (1.0 + sc_ref[...]) + sh_ref[...]
        hb = h.astype(BF16)
        h_ref[...] = hb
        sm = jnp.dot(hb, ws_ref[...], preferred_element_type=F32) + bs_ref[...]
        os_ref[...] = jnp.where(_iota(sm.shape, 1) < FOX_HEADS, _log_sigmoid(sm), sm)

    o_ref[...] = jnp.dot(h_ref[...], w_ref[...], preferred_element_type=F32)


def _in_proj(x, g, mod, per_row, rows_per_seq, w_big, w_small, b_small, tm, tn=512):
    t, d = x.shape
    tps = max(rows_per_seq // tm, 1)
    return pl.pallas_call(
        _inproj_kernel,
        grid=(t // tm, BIG_COLS // tn),
        in_specs=[pl.BlockSpec((tm, d), lambda i, j: (i, 0)),
                  pl.BlockSpec((1, d), lambda i, j: (0, 0)),
                  _mod_spec(per_row, 0, tm, d, tps),
                  _mod_spec(per_row, 1, tm, d, tps),
                  pl.BlockSpec((d, tn), lambda i, j: (0, j)),
                  pl.BlockSpec((d, SMALL_COLS), lambda i, j: (0, 0)),
                  pl.BlockSpec((1, SMALL_COLS), lambda i, j: (0, 0))],
        out_specs=[pl.BlockSpec((tm, tn), lambda i, j: (i, j)),
                   pl.BlockSpec((tm, SMALL_COLS), lambda i, j: (i, 0))],
        out_shape=[jax.ShapeDtypeStruct((t, BIG_COLS), F32),
                   jax.ShapeDtypeStruct((t, SMALL_COLS), F32)],
        scratch_shapes=[pltpu.VMEM((tm, d), BF16)],
        compiler_params=_params("parallel", "arbitrary"),
        name="in_proj",
    )(x, g, mod, mod, w_big, w_small, b_small)


def _cum_kernel(x_ref, o_ref, carry_ref):
    @pl.when(pl.program_id(1) == 0)
    def _():
        carry_ref[...] = jnp.zeros_like(carry_ref)

    x = x_ref[...]
    tl = x.shape[0]
    tri = (_iota((tl, tl), 1) <= _iota((tl, tl), 0)).astype(F32)
    cum = jnp.dot(tri, x, precision=HI, preferred_element_type=F32) + carry_ref[...]
    carry_ref[...] = cum[tl - 1:tl, :]
    o_ref[...] = cum.T[:FOX_HEADS, :]


def _cum_logf(small, tl):
    b, l, _ = small.shape
    return pl.pallas_call(
        _cum_kernel,
        grid=(b, l // tl),
        in_specs=[pl.BlockSpec((None, tl, SMALL_COLS), lambda i, t: (i, t, 0))],
        out_specs=pl.BlockSpec((None, FOX_HEADS, tl), lambda i, t: (i, 0, t)),
        out_shape=jax.ShapeDtypeStruct((b, FOX_HEADS, l), F32),
        scratch_shapes=[pltpu.VMEM((1, SMALL_COLS), F32)],
        compiler_params=_params("parallel", "arbitrary"),
        name="cum_logf",
    )(small)


def _fox_kernel(qi_tab, kj_tab, q_ref, k_ref, v_ref, ck_ref, o_ref, qs_ref, m_ref, l_ref, acc_ref, *, tq):
    p_id = pl.program_id(2)
    qi = qi_tab[p_id]
    kj = kj_tab[p_id]
    scale = FOX_HEAD_DIM ** -0.5

    @pl.when(kj == 0)
    def _():
        m_ref[...] = jnp.full_like(m_ref, -jnp.inf)
        l_ref[...] = jnp.zeros_like(l_ref)
        acc_ref[...] = jnp.zeros_like(acc_ref)
        for g in range(FOX_GROUP):
            qs_ref[g * tq:(g + 1) * tq, :] = (q_ref[:, g * FOX_HEAD_DIM:(g + 1) * FOX_HEAD_DIM] * scale).astype(BF16)

    def step(masked):
        kb = k_ref[...].astype(BF16)
        vb = v_ref[...].astype(BF16)
        s = lax.dot_general(qs_ref[...], kb, NT_DIMS, preferred_element_type=F32)
        ck = ck_ref[...]
        tk = kb.shape[0]
        parts = []
        for g in range(FOX_GROUP):
            sg = s[g * tq:(g + 1) * tq, :] - ck[g:g + 1, :]
            if masked:
                sg = jnp.where(_iota((tq, tk), 1) <= _iota((tq, tk), 0), sg, -jnp.inf)
            parts.append(sg)
        s = jnp.concatenate(parts, axis=0)
        m_prev = m_ref[...]
        m_next = jnp.maximum(m_prev, jnp.max(s, axis=1, keepdims=True))
        p = jnp.exp(s - m_next[:, :1])
        alpha = jnp.exp(m_prev - m_next)
        l_ref[...] = alpha * l_ref[...] + jnp.sum(p, axis=1, keepdims=True)
        m_ref[...] = m_next
        acc_ref[...] = alpha * acc_ref[...] + jnp.dot(p.astype(BF16), vb, preferred_element_type=F32)

    @pl.when(kj < qi)
    def _():
        step(False)

    @pl.when(kj == qi)
    def _():
        step(True)
        o = acc_ref[...] / l_ref[...]
        for g in range(FOX_GROUP):
            o_ref[:, g * FOX_HEAD_DIM:(g + 1) * FOX_HEAD_DIM] = o[g * tq:(g + 1) * tq, :].astype(o_ref.dtype)


def _fox_prompt(proj, cum4, tq):
    b, l, _ = proj.shape
    nq = l // tq
    pairs = [(i, j) for i in range(nq) for j in range(i + 1)]
    qi_tab = jnp.asarray(np.array([p[0] for p in pairs], np.int32))
    kj_tab = jnp.asarray(np.array([p[1] for p in pairs], np.int32))
    gw = FOX_GROUP * FOX_HEAD_DIM
    kcol = COL_FK // FOX_HEAD_DIM
    vcol = COL_FV // FOX_HEAD_DIM
    grid_spec = pltpu.PrefetchScalarGridSpec(
        num_scalar_prefetch=2,
        grid=(b, FOX_KV_HEADS, len(pairs)),
        in_specs=[pl.BlockSpec((None, tq, gw), lambda i, h, p, qt, kt: (i, qt[p], h)),
                  pl.BlockSpec((None, tq, FOX_HEAD_DIM), lambda i, h, p, qt, kt: (i, kt[p], kcol + h)),
                  pl.BlockSpec((None, tq, FOX_HEAD_DIM), lambda i, h, p, qt, kt: (i, kt[p], vcol + h)),
                  pl.BlockSpec((None, None, FOX_GROUP, tq), lambda i, h, p, qt, kt: (i, h, 0, kt[p]))],
        out_specs=pl.BlockSpec((None, tq, gw), lambda i, h, p, qt, kt: (i, qt[p], h)),
        scratch_shapes=[pltpu.VMEM((FOX_GROUP * tq, FOX_HEAD_DIM), BF16),
                        pltpu.VMEM((FOX_GROUP * tq, FOX_HEAD_DIM), F32),
                        pltpu.VMEM((FOX_GROUP * tq, FOX_HEAD_DIM), F32),
                        pltpu.VMEM((FOX_GROUP * tq, FOX_HEAD_DIM), F32)],
    )
    return pl.pallas_call(
        functools.partial(_fox_kernel, tq=tq),
        grid_spec=grid_spec,
        out_shape=jax.ShapeDtypeStruct((b, l, FOX_WIDTH), BF16),
        compiler_params=_params("parallel", "parallel", "arbitrary"),
        name="fox_prompt",
    )(qi_tab, kj_tab, proj, proj, proj, cum4)


def _foxdec_kernel(pt_ref, q_ref, kn_ref, vn_ref, lfn_ref, *refs, n_pages):
    k_refs = refs[:n_pages]
    v_refs = refs[n_pages:2 * n_pages]
    lf_refs = refs[2 * n_pages:3 * n_pages]
    o_ref = refs[3 * n_pages]
    del pt_ref
    scale = FOX_HEAD_DIM ** -0.5
    page = k_refs[0].shape[0]
    q = q_ref[...] * scale
    qb = q.astype(BF16)
    rowgrp = _iota((FOX_HEADS, page), 0) // FOX_GROUP
    eye = (_iota((FOX_HEADS, FOX_HEADS), 0) == _iota((FOX_HEADS, FOX_HEADS), 1)).astype(F32)
    tri = (_iota((page, page), 1) <= _iota((page, page), 0)).astype(F32)

    off = jnp.zeros((1, FOX_HEADS), F32)
    logits = []
    for p in range(n_pages):
        cum = jnp.dot(tri, lf_refs[p][...], precision=HI, preferred_element_type=F32) + off
        off = cum[page - 1:page, :]
        cum_t = lax.dot_general(eye, cum, NT_DIMS, precision=HI, preferred_element_type=F32)
        sp = jnp.zeros((FOX_HEADS, page), F32)
        for h in range(FOX_KV_HEADS):
            kh = k_refs[p][:, h, :].astype(BF16)
            sp = jnp.where(rowgrp == h, lax.dot_general(qb, kh, NT_DIMS, preferred_element_type=F32), sp)
        logits.append(sp - cum_t)
        tot = cum_t[:, page - 1:page]
    s_new = jnp.sum(q * kn_ref[...], axis=1, keepdims=True) - (tot + lfn_ref[...])

    m = s_new
    for lg in logits:
        m = jnp.maximum(m, jnp.max(lg, axis=1, keepdims=True))
    p_new = jnp.exp(s_new - m)
    l = p_new
    acc = p_new * vn_ref[...]
    for p in range(n_pages):
        pp = jnp.exp(logits[p] - m)
        l = l + jnp.sum(pp, axis=1, keepdims=True)
        for h in range(FOX_KV_HEADS):
            ph = jnp.where(rowgrp == h, pp, 0.0).astype(BF16)
            acc = acc + jnp.dot(ph, v_refs[p][:, h, :].astype(BF16), preferred_element_type=F32)
    o_ref[...] = (acc / l).astype(o_ref.dtype)


def _fox_decode(q, k_new, v_new, lf_new, cache_k, cache_v, cache_logf, page_table):
    db, n_pages = page_table.shape
    page = cache_k.shape[2]

    def page_spec(p, tail):
        return pl.BlockSpec((None, None, page) + tail, lambda i, pt: (0, pt[i, p]) + (0,) * (1 + len(tail)))

    row = pl.BlockSpec((None, FOX_HEADS, FOX_HEAD_DIM), lambda i, pt: (i, 0, 0))
    in_specs = [row, row, row, pl.BlockSpec((None, FOX_HEADS, 1), lambda i, pt: (i, 0, 0))]
    in_specs += [page_spec(p, (FOX_KV_HEADS, FOX_HEAD_DIM)) for p in range(n_pages)]
    in_specs += [page_spec(p, (FOX_KV_HEADS, FOX_HEAD_DIM)) for p in range(n_pages)]
    in_specs += [page_spec(p, (FOX_HEADS,)) for p in range(n_pages)]
    grid_spec = pltpu.PrefetchScalarGridSpec(
        num_scalar_prefetch=1, grid=(db,), in_specs=in_specs,
        out_specs=pl.BlockSpec((None, FOX_HEADS, FOX_HEAD_DIM), lambda i, pt: (i, 0, 0)))
    return pl.pallas_call(
        functools.partial(_foxdec_kernel, n_pages=n_pages),
        grid_spec=grid_spec,
        out_shape=jax.ShapeDtypeStruct((db, FOX_HEADS, FOX_HEAD_DIM), BF16),
        compiler_params=_params("parallel"),
        name="fox_decode",
    )(page_table, q, k_new, v_new, lf_new, *([cache_k] * n_pages), *([cache_v] * n_pages),
      *([cache_logf] * n_pages))


def _gla_gate(gg, wup, bg):
    return _log_sigmoid(jnp.dot(gg, wup, precision=HI, preferred_element_type=F32) + bg) / GLA_GATE_TEMP


def _gla_out(o, g, gr):
    return _rms(o, g) * _silu(gr)


def _gla_kernel(q_ref, k_ref, v_ref, r_ref, sm_ref, wup_ref, bg_ref, gn_ref, o_ref, s_ref):
    @pl.when(pl.program_id(2) == 0)
    def _():
        s_ref[...] = jnp.zeros_like(s_ref)

    c = q_ref.shape[0]
    q = q_ref[...] * (GLA_DK ** -0.5)
    k = k_ref[...]
    v = v_ref[...]
    gg = sm_ref[:, FOX_HEADS:FOX_HEADS + GLA_GATE_RANK]
    la = _gla_gate(gg, wup_ref[...], bg_ref[...])
    tri = (_iota((c, c), 1) <= _iota((c, c), 0)).astype(F32)
    b = jnp.dot(tri, la, precision=HI, preferred_element_type=F32)
    b_last = b[c - 1:c, :]
    s_old = s_ref[...]
    vb = v.astype(BF16)
    o = jnp.dot((q * jnp.exp(b)).astype(BF16), s_old.astype(BF16), preferred_element_type=F32)

    rows = _iota((c, GLA_DK), 0)
    lane = _iota((c, c), 1)
    a = jnp.zeros((c, c), F32)
    for j in range(c):
        w = jnp.exp(jnp.where(rows >= j, b - b[j:j + 1, :], -jnp.inf))
        col = jnp.sum(q * w * k[j:j + 1, :], axis=1, keepdims=True)
        a = jnp.where(lane == j, col, a)
    o = o + jnp.dot(a.astype(BF16), vb, preferred_element_type=F32)
    o_ref[...] = _gla_out(o, gn_ref[...], r_ref[...]).astype(o_ref.dtype)

    kd = (k * jnp.exp(b_last - b)).astype(BF16)
    ones = jnp.ones((c, LANES), F32)
    decay = jnp.exp(lax.dot_general(la, ones, TN_DIMS, precision=HI, preferred_element_type=F32))
    decay = jnp.concatenate([decay] * (GLA_DV // LANES), axis=1)
    s_ref[...] = decay * s_old + lax.dot_general(kd, vb, TN_DIMS, preferred_element_type=F32)


def _gla_prompt(proj, small, wup, bg, gn):
    b, l, _ = proj.shape
    c = GLA_CHUNK if l % GLA_CHUNK == 0 else l
    qc, kc = COL_GQ // GLA_DK, COL_GK // GLA_DK
    vc, rc = COL_GV // GLA_DV, COL_GR // GLA_DV
    return pl.pallas_call(
        _gla_kernel,
        grid=(b, GLA_HEADS, l // c),
        in_specs=[pl.BlockSpec((None, c, GLA_DK), lambda i, h, t: (i, t, qc + h)),
                  pl.BlockSpec((None, c, GLA_DK), lambda i, h, t: (i, t, kc + h)),
                  pl.BlockSpec((None, c, GLA_DV), lambda i, h, t: (i, t, vc + h)),
                  pl.BlockSpec((None, c, GLA_DV), lambda i, h, t: (i, t, rc + h)),
                  pl.BlockSpec((None, c, SMALL_COLS), lambda i, h, t: (i, t, 0)),
                  pl.BlockSpec((GLA_GATE_RANK, GLA_DK), lambda i, h, t: (0, h)),
                  pl.BlockSpec((1, GLA_DK), lambda i, h, t: (0, h)),
                  pl.BlockSpec((1, GLA_DV), lambda i, h, t: (0, 0))],
        out_specs=[pl.BlockSpec((None, c, GLA_DV), lambda i, h, t: (i, t, h)),
                   pl.BlockSpec((None, None, GLA_DK, GLA_DV), lambda i, h, t: (i, h, 0, 0))],
        out_shape=[jax.ShapeDtypeStruct((b, l, GLA_WIDTH), BF16),
                   jax.ShapeDtypeStruct((b, GLA_HEADS, GLA_DK, GLA_DV), F32)],
        compiler_params=_params("parallel", "parallel", "arbitrary"),
        name="gla_prompt",
    )(proj, proj, proj, proj, small, wup, bg, gn)


def _gladec_kernel(q_ref, k_ref, v_ref, r_ref, sm_ref, wup_ref, bg_ref, gn_ref, s_ref, o_ref, so_ref):
    gg = sm_ref[:, FOX_HEADS:FOX_HEADS + GLA_GATE_RANK]
    la = _gla_gate(gg, wup_ref[...], bg_ref[...])
    eye = (_iota((GLA_DK, GLA_DK), 0) == _iota((GLA_DK, GLA_DK), 1)).astype(F32)
    for h in range(GLA_HEADS):
        ks = slice(h * GLA_DK, (h + 1) * GLA_DK)
        vs = slice(h * GLA_DV, (h + 1) * GLA_DV)
        rows3 = jnp.concatenate([jnp.exp(la[:, ks]), k_ref[:, ks], q_ref[:, ks] * (GLA_DK ** -0.5),
                                 jnp.zeros((SUBLANES - 3, GLA_DK), F32)], axis=0)
        cols = lax.dot_general(eye, rows3, NT_DIMS, precision=HI, preferred_element_type=F32)
        s_new = cols[:, 0:1] * s_ref[h] + cols[:, 1:2] * v_ref[:, vs]
        so_ref[h] = s_new
        o = jnp.sum(cols[:, 2:3] * s_new, axis=0, keepdims=True)
        o_ref[:, vs] = _gla_out(o, gn_ref[...], r_ref[:, vs]).astype(o_ref.dtype)


def _gla_decode(gq, gk, gv, gr, small3, wup, bg, gn, state):
    db = gq.shape[0]
    st =pl.BlockSpec((None, GLA_HEADS, GLA_DK, GLA_DV), lambda i: (i, 0, 0, 0))
    return pl.pallas_call(
        _gladec_kernel,
        grid=(db,),
        in_specs=[pl.BlockSpec((None, 1, GLA_QK_WIDTH), lambda i: (i, 0, 0)),
                  pl.BlockSpec((None, 1, GLA_QK_WIDTH), lambda i: (i, 0, 0)),
                  pl.BlockSpec((None, 1, GLA_WIDTH), lambda i: (i, 0, 0)),
                  pl.BlockSpec((None, 1, GLA_WIDTH), lambda i: (i, 0, 0)),
                  pl.BlockSpec((None, 1, SMALL_COLS), lambda i: (i, 0, 0)),
                  pl.BlockSpec((GLA_GATE_RANK, GLA_QK_WIDTH), lambda i: (0, 0)),
                  pl.BlockSpec((1, GLA_QK_WIDTH), lambda i: (0, 0)),
                  pl.BlockSpec((1, GLA_DV), lambda i: (0, 0)),
                  st],
        out_specs=[pl.BlockSpec((None, 1, GLA_WIDTH), lambda i: (i, 0, 0)), st],
        out_shape=[jax.ShapeDtypeStruct((db, 1, GLA_WIDTH), BF16),
                   jax.ShapeDtypeStruct(state.shape, F32)],
        compiler_params=_params("parallel"),
        name="gla_decode",
    )(gq, gk, gv, gr, small3, wup, bg, gn, state)


def _outproj_kernel(a_ref, b_ref, wa_ref, wb_ref, x_ref, g_ref, gt_ref, o_ref, acc_ref):
    j = pl.program_id(1)
    nj = pl.num_programs(1)
    acc_ref[j] = (jnp.dot(a_ref[...], wa_ref[...], preferred_element_type=F32)
                  + jnp.dot(b_ref[...], wb_ref[...], preferred_element_type=F32))

    @pl.when(j == nj - 1)
    def _():
        n_chunks, _, tn = acc_ref.shape
        ss = jnp.zeros((acc_ref.shape[1], 1), F32)
        for c in range(n_chunks):
            z = acc_ref[c]
            ss = ss + jnp.sum(z * z, axis=1, keepdims=True)
        rs = lax.rsqrt(ss / (n_chunks * tn) + NORM_EPS)
        for c in range(n_chunks):
            cs = slice(c * tn, (c + 1) * tn)
            o_ref[:, cs] = x_ref[:, cs] + gt_ref[:, cs] * (acc_ref[c] * rs * g_ref[:, cs])


def _out_proj(o_fox, o_gla, w_out, x, g, mod, per_row, rows_per_seq, tm, tn=512):
    t, d = x.shape
    ka = o_fox.shape[1]
    tn = min(tn, d)
    tps = max(rows_per_seq // tm, 1)
    return pl.pallas_call(
        _outproj_kernel,
        grid=(t // tm, d // tn),
        in_specs=[pl.BlockSpec((tm, ka), lambda i, j: (i, 0)),
                  pl.BlockSpec((tm, ka), lambda i, j: (i, 0)),
                  pl.BlockSpec((ka, tn), lambda i, j: (0, j)),
                  pl.BlockSpec((ka, tn), lambda i, j: (1, j)),
                  pl.BlockSpec((tm, d), lambda i, j: (i, 0)),
                  pl.BlockSpec((1, d), lambda i, j: (0, 0)),
                  _mod_spec(per_row, 2, tm, d, tps)],
        out_specs=pl.BlockSpec((tm, d), lambda i, j: (i, 0)),
        out_shape=jax.ShapeDtypeStruct((t, d), F32),
        scratch_shapes=[pltpu.VMEM((d // tn, tm, tn), F32)],
        compiler_params=_params("parallel", "arbitrary"),
        name="out_proj",
    )(o_fox, o_gla, w_out, w_out, x, g, mod)


def _ffnpre_kernel(x_ref, g_ref, sh_ref, sc_ref, wr_ref, t_ref, lg_ref):
    h = _rms(x_ref[...], g_ref[...]) * (1.0 + sc_ref[...]) + sh_ref[...]
    t_ref[...] = h.astype(BF16)
    lg_ref[...] = jnp.dot(h, wr_ref[...], precision=HI, preferred_element_type=F32)


def _ffn_pre(x, g, mod, per_row, rows_per_seq, w_router, tm):
    t, d = x.shape
    tps = max(rows_per_seq // tm, 1)
    return pl.pallas_call(
        _ffnpre_kernel,
        grid=(t // tm,),
        in_specs=[pl.BlockSpec((tm, d), lambda i: (i, 0)),
                  pl.BlockSpec((1, d), lambda i: (0, 0)),
                  _mod_spec(per_row, 3, tm, d, tps),
                  _mod_spec(per_row, 4, tm, d, tps),
                  pl.BlockSpec((d, N_EXPERTS), lambda i: (0, 0))],
        out_specs=[pl.BlockSpec((tm, d), lambda i: (i, 0)),
                   pl.BlockSpec((tm, N_EXPERTS), lambda i: (i, 0))],
        out_shape=[jax.ShapeDtypeStruct((t, d), BF16),
                   jax.ShapeDtypeStruct((t, N_EXPERTS), F32)],
        compiler_params=_params("parallel"),
        name="ffn_pre",
    )(x, g, mod, mod, w_router)


def _new_expert(be_ref, i):
    return jnp.logical_or(i == 0, be_ref[i] != be_ref[jnp.maximum(i - 1, 0)])


def _gateup_kernel(be_ref, nu_ref, x_ref, wg_ref, wu_ref, h_ref, wgb_ref, wub_ref):
    i = pl.program_id(1)

    @pl.when(jnp.logical_and(i < nu_ref[0], _new_expert(be_ref, i)))
    def _():
        wgb_ref[...] = wg_ref[...].astype(BF16)
        wub_ref[...] = wu_ref[...].astype(BF16)

    @pl.when(i < nu_ref[0])
    def _():
        x = x_ref[...]
        g = jnp.dot(x, wgb_ref[...], preferred_element_type=F32)
        u = jnp.dot(x, wub_ref[...], preferred_element_type=F32)
        h_ref[...] = (_silu(g) * u).astype(h_ref.dtype)


def _expert_gate_up(block_expert, n_used, x_sorted, w_gate, w_up, tm, tf=512):
    r, d = x_sorted.shape
    ff = w_gate.shape[-1]
    wspec = pl.BlockSpec((None, None, d, tf), lambda f, i, be, nu: (0, be[i], 0, f))
    grid_spec = pltpu.PrefetchScalarGridSpec(
        num_scalar_prefetch=2, grid=(ff // tf, r // tm),
        in_specs=[pl.BlockSpec((tm, d), lambda f, i, be, nu: (i, 0)), wspec, wspec],
        out_specs=pl.BlockSpec((tm, tf), lambda f, i, be, nu: (i, f)),
        scratch_shapes=[pltpu.VMEM((d, tf), BF16), pltpu.VMEM((d, tf), BF16)])
    return pl.pallas_call(
        _gateup_kernel, grid_spec=grid_spec,
        out_shape=jax.ShapeDtypeStruct((r, ff), BF16),
        compiler_params=_params("arbitrary", "arbitrary"),
        name="expert_gate_up",
    )(block_expert, n_used, x_sorted, w_gate, w_up)


def _down_kernel(be_ref, nu_ref, h_ref, wd_ref, y_ref, wdb_ref):
    i = pl.program_id(1)

    @pl.when(jnp.logical_and(i < nu_ref[0], _new_expert(be_ref, i)))
    def _():
        wdb_ref[...] = wd_ref[...].astype(BF16)

    @pl.when(i < nu_ref[0])
    def _():
        y_ref[...] = jnp.dot(h_ref[...], wdb_ref[...], preferred_element_type=F32)


def _expert_down(block_expert, n_used, h_sorted, w_down, tm, tn=1024):
    r, ff = h_sorted.shape
    d = w_down.shape[-1]
    tn = min(tn, d)
    grid_spec = pltpu.PrefetchScalarGridSpec(
        num_scalar_prefetch=2, grid=(d // tn, r // tm),
        in_specs=[pl.BlockSpec((tm, ff), lambda n, i, be, nu: (i, 0)),
                  pl.BlockSpec((None, None, ff, tn), lambda n, i, be, nu: (0, be[i], 0, n))],
        out_specs=pl.BlockSpec((tm, tn), lambda n, i, be, nu: (i, n)),
        scratch_shapes=[pltpu.VMEM((ff, tn), BF16)])
    return pl.pallas_call(
        _down_kernel, grid_spec=grid_spec,
        out_shape=jax.ShapeDtypeStruct((r, d), F32),
        compiler_params=_params("arbitrary", "arbitrary"),
        name="expert_down",
    )(block_expert, n_used, h_sorted, w_down)


def _shared_kernel(t_ref, wg_ref, wu_ref, wd_ref, rt_ref, x_ref, g_ref, gt_ref, o_ref, acc_ref):
    f = pl.program_id(1)

    @pl.when(f == 0)
    def _():
        acc_ref[...] = rt_ref[...]

    t = t_ref[...]
    h = _silu(jnp.dot(t, wg_ref[...], preferred_element_type=F32)) * jnp.dot(t, wu_ref[...], preferred_element_type=F32)
    acc_ref[...] += jnp.dot(h.astype(BF16), wd_ref[...], preferred_element_type=F32)

    @pl.when(f == pl.num_programs(1) - 1)
    def _():
        o_ref[...] = x_ref[...] + gt_ref[...] * _rms(acc_ref[...], g_ref[...])


def _shared_final(t, wg, wu, wd, routed, x, g, mod, per_row, rows_per_seq, tm, tf=256):
    n, d = x.shape
    ff = wg.shape[1]
    tps = max(rows_per_seq // tm, 1)
    return pl.pallas_call(
        _shared_kernel,
        grid=(n // tm, ff // tf),
        in_specs=[pl.BlockSpec((tm, d), lambda i, f: (i, 0)),
                  pl.BlockSpec((d, tf), lambda i, f: (0, f)),
                  pl.BlockSpec((d, tf), lambda i, f: (0, f)),
                  pl.BlockSpec((tf, d), lambda i, f: (f, 0)),
                  pl.BlockSpec((tm, d), lambda i, f: (i, 0)),
                  pl.BlockSpec((tm, d), lambda i, f: (i, 0)),
                  pl.BlockSpec((1, d), lambda i, f: (0, 0)),
                  _mod_spec(per_row, 5, tm, d, tps)],
        out_specs=pl.BlockSpec((tm, d), lambda i, f: (i, 0)),
        out_shape=jax.ShapeDtypeStruct((n, d), F32),
        scratch_shapes=[pltpu.VMEM((tm, d), F32)],
        compiler_params=_params("parallel", "arbitrary"),
        name="shared_final",
    )(t, wg, wu, wd, routed, x, g, mod)


def _route(logits, b_router):
    t = logits.shape[0]
    scores = jax.nn.sigmoid(logits)
    sel = scores + b_router.astype(F32)
    grp = lax.top_k(sel.reshape(t, N_GROUPS, EXPERTS_PER_GROUP), 2)[0].sum(-1)
    _, top_g = lax.top_k(grp, TOPK_GROUPS)
    gmask = jnp.any(top_g[:, :, None] == jnp.arange(N_GROUPS)[None, None, :], axis=1)
    emask = jnp.repeat(gmask, EXPERTS_PER_GROUP, axis=1)
    _, idx = lax.top_k(jnp.where(emask, sel, -jnp.inf), TOP_K)
    w = jnp.take_along_axis(scores, idx, axis=1)
    w = w / jnp.sum(w, axis=-1, keepdims=True) * ROUTED_SCALE
    return idx.astype(jnp.int32), w


def _dispatch(idx, tm):
    t = idx.shape[0]
    m = t * TOP_K
    onehot = jnp.sum((idx[:, :, None] == jnp.arange(N_EXPERTS, dtype=jnp.int32)[None, None, :]).astype(jnp.int32), axis=1)
    excl = jnp.cumsum(onehot, axis=0) - onehot
    counts = jnp.sum(onehot, axis=0)
    padded = (counts + tm - 1) // tm * tm
    pad_end = jnp.cumsum(padded)
    pad_start = pad_end - padded
    dest = pad_start[idx] + jnp.take_along_axis(excl, idx, axis=1)
    n_blocks = -(-(m + N_EXPERTS * (tm - 1)) // tm)
    slot_tok = jnp.zeros((n_blocks * tm,), jnp.int32).at[dest.reshape(-1)].set(
        jnp.repeat(jnp.arange(t, dtype=jnp.int32), TOP_K))
    block_expert = jnp.minimum(
        jnp.searchsorted(pad_end, jnp.arange(n_blocks, dtype=jnp.int32) * tm, side='right'), N_EXPERTS - 1)
    n_used = (pad_end[-1] // tm).astype(jnp.int32).reshape(1)
    return dest.astype(jnp.int32), slot_tok, block_expert.astype(jnp.int32), n_used


def _tile(n, pref):
    return pref if n % pref == 0 else n


def kernel(x_prompt, x_sample, c_prompt, c_sample, cache_k, cache_v, cache_logf, state_gla, page_table, w_ada, b_ada, g_attn_pre, g_attn_post, g_ffn_pre, g_ffn_post, w_in, b_forget, w_gla_gate_up, b_gla_gate, g_gla_out, w_out, w_router, b_router, w_e_gate, w_e_up, w_e_down, w_s_gate, w_s_up, w_s_down):
    bsz, seq, d = x_prompt.shape
    db = x_sample.shape[0]
    depth = w_ada.shape[0]
    tp, ts = bsz * seq, db
    y_p = x_prompt.reshape(tp, d)
    y_s = x_sample.reshape(ts, d)
    outs = [[] for _ in range(8)]

    n_mod = bsz + db
    mp = -(-n_mod // SUBLANES) * SUBLANES
    c_all = jnp.concatenate([c_prompt, c_sample, jnp.zeros((mp - n_mod, d), F32)], axis=0)

    o_fg = FOX_WIDTH + 2 * FOX_KV_WIDTH
    o_gq = o_fg + FOX_HEADS
    o_gg = o_gq + 2 * GLA_QK_WIDTH + GLA_WIDTH
    o_gr = o_gg + GLA_GATE_RANK

    for l in range(depth):
        mod = _modulation(c_all, w_ada[l], b_ada[l])
        mod_p = mod[:bsz].reshape(bsz, 6, 1, d)
        mod_s = mod[bsz:n_mod].reshape(db, 6, d).transpose(1, 0, 2)

        wl = w_in[l]
        w_big = jnp.concatenate([wl[:, :o_fg], wl[:, o_gq:o_gg], wl[:, o_gr:]], axis=1).astype(BF16)
        w_small = jnp.concatenate([wl[:, o_fg:o_gq], wl[:, o_gg:o_gr],
                                   jnp.zeros((d, SMALL_COLS - FOX_HEADS - GLA_GATE_RANK), F32)], axis=1).astype(BF16)
        b_small = jnp.concatenate([b_forget[l], jnp.zeros((SMALL_COLS - FOX_HEADS,), F32)]).reshape(1, SMALL_COLS)
        w_out_b = w_out[l].astype(BF16)
        wup = w_gla_gate_up[l]
        bg = b_gla_gate[l].reshape(1, GLA_QK_WIDTH)
        gn = g_gla_out[l].reshape(1, GLA_DV)
        g_ap = g_attn_pre[l].reshape(1, d)
        g_ao = g_attn_post[l].reshape(1, d)
        g_fp = g_ffn_pre[l].reshape(1, d)
        g_fo = g_ffn_post[l].reshape(1, d)

        tm_p = _tile(seq, 512)
        proj, small = _in_proj(y_p, g_ap, mod_p, False, seq, w_big, w_small, b_small, tm_p)
        proj3 = proj.reshape(bsz, seq, BIG_COLS)
        small3 = small.reshape(bsz, seq, SMALL_COLS)
        cum = _cum_logf(small3, _tile(seq, 512))
        o_fox = _fox_prompt(proj3, cum.reshape(bsz, FOX_KV_HEADS, FOX_GROUP, seq), _tile(seq, 512))
        o_gla, s_fin = _gla_prompt(proj3, small3, wup, bg, gn)
        outs[0].append(proj3[:, :, COL_FK:COL_FV].reshape(bsz, seq, FOX_KV_HEADS, FOX_HEAD_DIM))
        outs[1].append(proj3[:, :, COL_FV:COL_GQ].reshape(bsz, seq, FOX_KV_HEADS, FOX_HEAD_DIM))
        outs[2].append(small3[:, :, :FOX_HEADS])
        outs[3].append(s_fin)
        y_p = _out_proj(o_fox.reshape(tp, FOX_WIDTH), o_gla.reshape(tp, GLA_WIDTH), w_out_b, y_p, g_ao,
                        mod_p, False, seq, _tile(seq, 256))

        proj_s, small_s = _in_proj(y_s, g_ap, mod_s, True, 1, w_big, w_small, b_small, ts)
        k_new = proj_s[:, COL_FK:COL_FV].reshape(db, FOX_KV_HEADS, FOX_HEAD_DIM)
        v_new = proj_s[:, COL_FV:COL_GQ].reshape(db, FOX_KV_HEADS, FOX_HEAD_DIM)
        lf_new = small_s[:, :FOX_HEADS]
        o_fox_s = _fox_decode(proj_s[:, :FOX_WIDTH].reshape(db, FOX_HEADS, FOX_HEAD_DIM),
                              jnp.repeat(k_new, FOX_GROUP, axis=1), jnp.repeat(v_new, FOX_GROUP, axis=1),
                              lf_new.reshape(db, FOX_HEADS, 1), cache_k[l:l + 1], cache_v[l:l + 1],
                              cache_logf[l:l + 1], page_table)
        o_gla_s, s_new = _gla_decode(proj_s[:, COL_GQ:COL_GK].reshape(db, 1, GLA_QK_WIDTH),
                                     proj_s[:, COL_GK:COL_GV].reshape(db, 1, GLA_QK_WIDTH),
                                     proj_s[:, COL_GV:COL_GR].reshape(db, 1, GLA_WIDTH),
                                     proj_s[:, COL_GR:].reshape(db, 1, GLA_WIDTH),
                                     small_s.reshape(db, 1, SMALL_COLS), wup, bg, gn, state_gla[l])
        outs[4].append(k_new.reshape(db, 1, FOX_KV_HEADS, FOX_HEAD_DIM))
        outs[5].append(v_new.reshape(db, 1, FOX_KV_HEADS, FOX_HEAD_DIM))
        outs[6].append(lf_new.reshape(db, 1, FOX_HEADS))
        outs[7].append(s_new)
        y_s = _out_proj(o_fox_s.reshape(ts, FOX_WIDTH), o_gla_s.reshape(ts, GLA_WIDTH), w_out_b, y_s, g_ao,
                        mod_s, True, 1, ts)

        t_p, lg_p = _ffn_pre(y_p, g_fp, mod_p, False, seq, w_router[l], _tile(seq, 512))
        t_s, lg_s = _ffn_pre(y_s, g_fp, mod_s, True, 1, w_router[l], ts)
        t_all = jnp.concatenate([t_p, t_s], axis=0)
        idx, wts = _route(jnp.concatenate([lg_p, lg_s], axis=0), b_router[l])
        tm_e = 256
        dest, slot_tok, block_expert, n_used = _dispatch(idx, tm_e)
        x_sorted = jnp.take(t_all, slot_tok, axis=0)
        h_sorted = _expert_gate_up(block_expert, n_used, x_sorted, w_e_gate[l:l + 1], w_e_up[l:l + 1], tm_e)
        y_sorted = _expert_down(block_expert, n_used, h_sorted, w_e_down[l:l + 1], tm_e)
        routed = jnp.sum(jnp.take(y_sorted, dest, axis=0) * wts[:, :, None], axis=1)
        wsg, wsu, wsd = w_s_gate[l].astype(BF16), w_s_up[l].astype(BF16), w_s_down[l].astype(BF16)
        y_p = _shared_final(t_p, wsg, wsu, wsd, routed[:tp], y_p, g_fo, mod_p, False, seq, _tile(seq, 256))
        y_s = _shared_final(t_s, wsg, wsu, wsd, routed[tp:], y_s, g_fo, mod_s, True, 1, ts)

    k_p, v_p, f_p, s_p, k_s, v_s, f_s, s_s = [jnp.stack(o) for o in outs]
    return (y_p.reshape(bsz, seq, d), y_s.reshape(db, 1, d), k_p, v_p, f_p, s_p, k_s, v_s, f_s, s_s)
```

```python
import functools

import numpy as np
import jax
import jax.numpy as jnp
from jax import lax
from jax.experimental import pallas as pl
from jax.experimental.pallas import tpu as pltpu

F32 = jnp.float32
BF16 = jnp.bfloat16
HI = lax.Precision.HIGHEST
NT_DIMS = (((1,), (1,)), ((), ()))
TN_DIMS = (((0,), (0,)), ((), ()))

FOX_HEADS = 16
FOX_KV_HEADS = 4
FOX_GROUP = FOX_HEADS // FOX_KV_HEADS
FOX_HEAD_DIM = 128
FOX_WIDTH = FOX_HEADS * FOX_HEAD_DIM
FOX_KV_WIDTH = FOX_KV_HEADS * FOX_HEAD_DIM
GLA_HEADS = 4
GLA_DK = 256
GLA_DV = 512
GLA_QK_WIDTH = GLA_HEADS * GLA_DK
GLA_WIDTH = GLA_HEADS * GLA_DV
GLA_GATE_RANK = 16
GLA_GATE_TEMP = 16.0
GLA_CHUNK = 64
N_EXPERTS = 64
TOP_K = 8
N_GROUPS = 8
TOPK_GROUPS = 4
EXPERTS_PER_GROUP = N_EXPERTS // N_GROUPS
ROUTED_SCALE = 2.5
NORM_EPS = 1e-6

LANES = 128
SUBLANES = 8
VMEM_LIMIT_BYTES = 56 * 1024 * 1024

COL_FQ = 0
COL_FK = COL_FQ + FOX_WIDTH
COL_FV = COL_FK + FOX_KV_WIDTH
COL_GQ = COL_FV + FOX_KV_WIDTH
COL_GK = COL_GQ + GLA_QK_WIDTH
COL_GV = COL_GK + GLA_QK_WIDTH
COL_GR = COL_GV + GLA_WIDTH
BIG_COLS = COL_GR + GLA_WIDTH
SMALL_COLS = LANES


def _params(*sem):
    return pltpu.CompilerParams(dimension_semantics=sem, vmem_limit_bytes=VMEM_LIMIT_BYTES)


def _sigmoid(x):
    return 1.0 / (1.0 + jnp.exp(-x))


def _silu(x):
    return x * _sigmoid(x)


def _log_sigmoid(x):
    return jnp.minimum(x, 0.0) - jnp.log(1.0 + jnp.exp(-jnp.abs(x)))


def _rms(x, g):
    return x * lax.rsqrt(jnp.mean(x * x, axis=-1, keepdims=True) + NORM_EPS) * g


def _iota(shape, dim):
    return lax.broadcasted_iota(jnp.int32, shape, dim)


def _mod_kernel(c_ref, w_ref, b_ref, o_ref):
    a = _silu(c_ref[...]).astype(BF16)
    o_ref[...] = jnp.dot(a, w_ref[...].astype(BF16), preferred_element_type=F32) + b_ref[...]


def _modulation(c_all, w_ada, b_ada, tn=512):
    mp, d = c_all.shape
    n = w_ada.shape[1]
    return pl.pallas_call(
        _mod_kernel,
        grid=(n // tn,),
        in_specs=[pl.BlockSpec((mp, d), lambda j: (0, 0)),
                  pl.BlockSpec((d, tn), lambda j: (0, j)),
                  pl.BlockSpec((1, tn), lambda j: (0, j))],
        out_specs=pl.BlockSpec((mp, tn), lambda j: (0, j)),
        out_shape=jax.ShapeDtypeStruct((mp, n), F32),
        compiler_params=_params("parallel"),
        name="modulation",
    )(c_all, w_ada, b_ada.reshape(1, n))


def _mod_spec(per_row, which, tm, d, tiles_per_seq):
    if per_row:
        return pl.BlockSpec((None, tm, d), lambda i, *_: (which, i, 0))
    return pl.BlockSpec((None, None, 1, d), lambda i, *_: (i // tiles_per_seq, which, 0, 0))


def _inproj_kernel(x_ref, g_ref, sh_ref, sc_ref, w_ref, ws_ref, bs_ref, o_ref, os_ref, h_ref):
    @pl.when(pl.program_id(1) == 0)
    def _():
        h = _rms(x_ref[...], g_ref[...]) * (1.0 + sc_ref[...]) + sh_ref[...]
        hb = h.astype(BF16)
        h_ref[...] = hb
        sm = jnp.dot(hb, ws_ref[...], preferred_element_type=F32) + bs_ref[...]
        os_ref[...] = jnp.where(_iota(sm.shape, 1) < FOX_HEADS, _log_sigmoid(sm), sm)

    o_ref[...] = jnp.dot(h_ref[...], w_ref[...], preferred_element_type=F32)


def _in_proj(x, g, mod, per_row, rows_per_seq, w_big, w_small, b_small, tm, tn=512):
    t, d = x.shape
    tps = max(rows_per_seq // tm, 1)
    return pl.pallas_call(
        _inproj_kernel,
        grid=(t // tm, BIG_COLS // tn),
        in_specs=[pl.BlockSpec((tm, d), lambda i, j: (i, 0)),
                  pl.BlockSpec((1, d), lambda i, j: (0, 0)),
                  _mod_spec(per_row, 0, tm, d, tps),
                  _mod_spec(per_row, 1, tm, d, tps),
                  pl.BlockSpec((d, tn), lambda i, j: (0, j)),
                  pl.BlockSpec((d, SMALL_COLS), lambda i, j: (0, 0)),
                  pl.BlockSpec((1, SMALL_COLS), lambda i, j: (0, 0))],
        out_specs=[pl.BlockSpec((tm, tn), lambda i, j: (i, j)),
                   pl.BlockSpec((tm, SMALL_COLS), lambda i, j: (i, 0))],
        out_shape=[jax.ShapeDtypeStruct((t, BIG_COLS), F32),
                   jax.ShapeDtypeStruct((t, SMALL_COLS), F32)],
        scratch_shapes=[pltpu.VMEM((tm, d), BF16)],
        compiler_params=_params("parallel", "arbitrary"),
        name="in_proj",
    )(x, g, mod, mod, w_big, w_small, b_small)


def _cum_kernel(x_ref, o_ref, carry_ref):
    @pl.when(pl.program_id(1) == 0)
    def _():
        carry_ref[...] = jnp.zeros_like(carry_ref)

    x = x_ref[...]
    tl = x.shape[0]
    tri = (_iota((tl, tl), 1) <= _iota((tl, tl), 0)).astype(F32)
    cum = jnp.dot(tri, x, precision=HI, preferred_element_type=F32) + carry_ref[...]
    carry_ref[...] = cum[tl - 1:tl, :]
    o_ref[...] = cum.T[:FOX_HEADS, :]


def _cum_logf(small, tl):
    b, l, _ = small.shape
    return pl.pallas_call(
        _cum_kernel,
        grid=(b, l // tl),
        in_specs=[pl.BlockSpec((None, tl, SMALL_COLS), lambda i, t: (i, t, 0))],
        out_specs=pl.BlockSpec((None, FOX_HEADS, tl), lambda i, t: (i, 0, t)),
        out_shape=jax.ShapeDtypeStruct((b, FOX_HEADS, l), F32),
        scratch_shapes=[pltpu.VMEM((1, SMALL_COLS), F32)],
        compiler_params=_params("parallel", "arbitrary"),
        name="cum_logf",
    )(small)


def _fox_kernel(qi_tab, kj_tab, q_ref, k_ref, v_ref, ck_ref, o_ref, qs_ref, m_ref, l_ref, acc_ref, *, tq):
    p_id = pl.program_id(2)
    qi = qi_tab[p_id]
    kj = kj_tab[p_id]
    scale = FOX_HEAD_DIM ** -0.5

    @pl.when(kj == 0)
    def _():
        m_ref[...] = jnp.full_like(m_ref, -jnp.inf)
        l_ref[...] = jnp.zeros_like(l_ref)
        acc_ref[...] = jnp.zeros_like(acc_ref)
        for g in range(FOX_GROUP):
            qs_ref[g * tq:(g + 1) * tq, :] = (q_ref[:, g * FOX_HEAD_DIM:(g + 1) * FOX_HEAD_DIM] * scale).astype(BF16)

    def step(masked):
        kb = k_ref[...].astype(BF16)
        vb = v_ref[...].astype(BF16)
        s = lax.dot_general(qs_ref[...], kb, NT_DIMS, preferred_element_type=F32)
        ck = ck_ref[...]
        tk = kb.shape[0]
        parts = []
        for g in range(FOX_GROUP):
            sg = s[g * tq:(g + 1) * tq, :] - ck[g:g + 1, :]
            if masked:
                sg = jnp.where(_iota((tq, tk), 1) <= _iota((tq, tk), 0), sg, -jnp.inf)
            parts.append(sg)
        s = jnp.concatenate(parts, axis=0)
        m_prev = m_ref[...]
        m_next = jnp.maximum(m_prev, jnp.max(s, axis=1, keepdims=True))
        p = jnp.exp(s - m_next[:, :1])
        alpha = jnp.exp(m_prev - m_next)
        l_ref[...] = alpha * l_ref[...] + jnp.sum(p, axis=1, keepdims=True)
        m_ref[...] = m_next
        acc_ref[...] = alpha * acc_ref[...] + jnp.dot(p.astype(BF16), vb, preferred_element_type=F32)

    @pl.when(kj < qi)
    def _():
        step(False)

    @pl.when(kj == qi)
    def _():
        step(True)
        o = acc_ref[...] / l_ref[...]
        for g in range(FOX_GROUP):
            o_ref[:, g * FOX_HEAD_DIM:(g + 1) * FOX_HEAD_DIM] = o[g * tq:(g + 1) * tq, :].astype(o_ref.dtype)


def _fox_prompt(proj, cum4, tq):
    b, l, _ = proj.shape
    nq = l // tq
    pairs = [(i, j) for i in range(nq) for j in range(i + 1)]
    qi_tab = jnp.asarray(np.array([p[0] for p in pairs], np.int32))
    kj_tab = jnp.asarray(np.array([p[1] for p in pairs], np.int32))
    gw = FOX_GROUP * FOX_HEAD_DIM
    kcol = COL_FK // FOX_HEAD_DIM
    vcol = COL_FV // FOX_HEAD_DIM
    grid_spec = pltpu.PrefetchScalarGridSpec(
        num_scalar_prefetch=2,
        grid=(b, FOX_KV_HEADS, len(pairs)),
        in_specs=[pl.BlockSpec((None, tq, gw), lambda i, h, p, qt, kt: (i, qt[p], h)),
                  pl.BlockSpec((None, tq, FOX_HEAD_DIM), lambda i, h, p, qt, kt: (i, kt[p], kcol + h)),
                  pl.BlockSpec((None, tq, FOX_HEAD_DIM), lambda i, h, p, qt, kt: (i, kt[p], vcol + h)),
                  pl.BlockSpec((None, None, FOX_GROUP, tq), lambda i, h, p, qt, kt: (i, h, 0, kt[p]))],
        out_specs=pl.BlockSpec((None, tq, gw), lambda i, h, p, qt, kt: (i, qt[p], h)),
        scratch_shapes=[pltpu.VMEM((FOX_GROUP * tq, FOX_HEAD_DIM), BF16),
                        pltpu.VMEM((FOX_GROUP * tq, FOX_HEAD_DIM), F32),
                        pltpu.VMEM((FOX_GROUP * tq, FOX_HEAD_DIM), F32),
                        pltpu.VMEM((FOX_GROUP * tq, FOX_HEAD_DIM), F32)],
    )
    return pl.pallas_call(
        functools.partial(_fox_kernel, tq=tq),
        grid_spec=grid_spec,
        out_shape=jax.ShapeDtypeStruct((b, l, FOX_WIDTH), BF16),
        compiler_params=_params("parallel", "parallel", "arbitrary"),
        name="fox_prompt",
    )(qi_tab, kj_tab, proj, proj, proj, cum4)


def _foxdec_kernel(pt_ref, q_ref, kn_ref, vn_ref, lfn_ref, *refs, n_pages):
    k_refs = refs[:n_pages]
    v_refs = refs[n_pages:2 * n_pages]
    lf_refs = refs[2 * n_pages:3 * n_pages]
    o_ref = refs[3 * n_pages]
    del pt_ref
    scale = FOX_HEAD_DIM ** -0.5
    cols = k_refs[0].shape[0]
    page = cols // FOX_KV_HEADS
    q = q_ref[...] * scale
    qb = q.astype(BF16)
    valid = (_iota((FOX_HEADS, cols), 1) % FOX_KV_HEADS) == (_iota((FOX_HEADS, cols), 0) // FOX_GROUP)
    spread = (_iota((page, cols), 0) <= _iota((page, cols), 1) // FOX_KV_HEADS).astype(F32)

    off = jnp.zeros((FOX_HEADS, 1), F32)
    logits = []
    for p in range(n_pages):
        lf = lf_refs[p][...]
        cum = jnp.dot(lf, spread, precision=HI, preferred_element_type=F32) + off
        off = off + jnp.sum(lf, axis=1, keepdims=True)
        sp = lax.dot_general(qb, k_refs[p][...].astype(BF16), NT_DIMS, preferred_element_type=F32)
        logits.append(jnp.where(valid, sp - cum, -jnp.inf))
    s_new = jnp.sum(q * kn_ref[...], axis=1, keepdims=True) - (off + lfn_ref[...])

    m = s_new
    for lg in logits:
        m = jnp.maximum(m, jnp.max(lg, axis=1, keepdims=True))
    p_new = jnp.exp(s_new - m)
    l = p_new
    acc = p_new * vn_ref[...]
    for p in range(n_pages):
        pp = jnp.exp(logits[p] - m)
        l = l + jnp.sum(pp, axis=1, keepdims=True)
        acc = acc + jnp.dot(pp.astype(BF16), v_refs[p][...].astype(BF16), preferred_element_type=F32)
    o_ref[...] = (acc / l).astype(o_ref.dtype)


def _fox_decode(q, k_new, v_new, lf_new, cache_k, cache_v, cache_lft, page_table):
    db, n_pages = page_table.shape

    def page_spec(p, shape):
        return pl.BlockSpec((None,) + shape, lambda i, pt: (pt[i, p], 0, 0))

    row = pl.BlockSpec((None, FOX_HEADS, FOX_HEAD_DIM), lambda i, pt: (i, 0, 0))
    in_specs = [row, row, row, pl.BlockSpec((None, FOX_HEADS, 1), lambda i, pt: (i, 0, 0))]
    in_specs += [page_spec(p, cache_k.shape[1:]) for p in range(n_pages)]
    in_specs += [page_spec(p, cache_v.shape[1:]) for p in range(n_pages)]
    in_specs += [page_spec(p, cache_lft.shape[1:]) for p in range(n_pages)]
    grid_spec = pltpu.PrefetchScalarGridSpec(
        num_scalar_prefetch=1, grid=(db,), in_specs=in_specs,
        out_specs=pl.BlockSpec((None, FOX_HEADS, FOX_HEAD_DIM), lambda i, pt: (i, 0, 0)))
    return pl.pallas_call(
        functools.partial(_foxdec_kernel, n_pages=n_pages),
        grid_spec=grid_spec,
        out_shape=jax.ShapeDtypeStruct((db, FOX_HEADS, FOX_HEAD_DIM), BF16),
        compiler_params=_params("parallel"),
        name="fox_decode",
    )(page_table, q, k_new, v_new, lf_new, *([cache_k] * n_pages), *([cache_v] * n_pages),
      *([cache_lft] * n_pages))


def _gla_gate(gg, wup, bg):
    return _log_sigmoid(jnp.dot(gg, wup, precision=HI, preferred_element_type=F32) + bg) / GLA_GATE_TEMP


def _gla_out(o, g, gr):
    return _rms(o, g) * _silu(gr)


def _gla_kernel(q_ref, k_ref, v_ref, r_ref, sm_ref, wup_ref, bg_ref, gn_ref, o_ref, s_ref):
    @pl.when(pl.program_id(2) == 0)
    def _():
        s_ref[...] = jnp.zeros_like(s_ref)

    c = q_ref.shape[0]
    q = q_ref[...] * (GLA_DK ** -0.5)
    k = k_ref[...]
    v = v_ref[...]
    gg = sm_ref[:, FOX_HEADS:FOX_HEADS + GLA_GATE_RANK]
    la = _gla_gate(gg, wup_ref[...], bg_ref[...])
    tri = (_iota((c, c), 1) <= _iota((c, c), 0)).astype(F32)
    b = jnp.dot(tri, la, precision=HI, preferred_element_type=F32)
    b_last = b[c - 1:c, :]
    s_old = s_ref[...]
    vb = v.astype(BF16)
    o = jnp.dot((q * jnp.exp(b)).astype(BF16), s_old.astype(BF16), preferred_element_type=F32)

    rows = _iota((c, GLA_DK), 0)
    lane = _iota((c, c), 1)
    a = jnp.zeros((c, c), F32)
    for j in range(c):
        w = jnp.exp(jnp.where(rows >= j, b - b[j:j + 1, :], -jnp.inf))
        col = jnp.sum(q * w * k[j:j + 1, :], axis=1, keepdims=True)
        a = jnp.where(lane == j, col, a)
    o = o + jnp.dot(a.astype(BF16), vb, preferred_element_type=F32)
    o_ref[...] = _gla_out(o, gn_ref[...], r_ref[...]).astype(o_ref.dtype)

    kd = (k * jnp.exp(b_last - b)).astype(BF16)
    ones = jnp.ones((c, LANES), F32)
    decay = jnp.exp(lax.dot_general(la, ones, TN_DIMS, precision=HI, preferred_element_type=F32))
    decay = jnp.concatenate([decay] * (GLA_DV // LANES), axis=1)
    s_ref[...] = decay * s_old + lax.dot_general(kd, vb, TN_DIMS, preferred_element_type=F32)


def _gla_prompt(proj, small, wup, bg, gn):
    b, l, _ = proj.shape
    c = GLA_CHUNK if l % GLA_CHUNK == 0 else l
    qc, kc = COL_GQ // GLA_DK, COL_GK // GLA_DK
    vc, rc = COL_GV // GLA_DV, COL_GR // GLA_DV
    return pl.pallas_call(
        _gla_kernel,
        grid=(b, GLA_HEADS, l // c),
        in_specs=[pl.BlockSpec((None, c, GLA_DK), lambda i, h, t: (i, t, qc + h)),
                  pl.BlockSpec((None, c, GLA_DK), lambda i, h, t: (i, t, kc + h)),
                  pl.BlockSpec((None, c, GLA_DV), lambda i, h, t: (i, t, vc + h)),
                  pl.BlockSpec((None, c, GLA_DV), lambda i, h, t: (i, t, rc + h)),
                  pl.BlockSpec((None, c, SMALL_COLS), lambda i, h, t: (i, t, 0)),
                  pl.BlockSpec((GLA_GATE_RANK, GLA_DK), lambda i, h, t: (0, h)),
                  pl.BlockSpec((1, GLA_DK), lambda i, h, t: (0, h)),
                  pl.BlockSpec((1, GLA_DV), lambda i, h, t: (0, 0))],
        out_specs=[pl.BlockSpec((None, c, GLA_DV), lambda i, h, t: (i, t, h)),
                   pl.BlockSpec((None, None, GLA_DK, GLA_DV), lambda i, h, t: (i, h, 0, 0))],
        out_shape=[jax.ShapeDtypeStruct((b, l, GLA_WIDTH), BF16),
                   jax.ShapeDtypeStruct((b, GLA_HEADS, GLA_DK, GLA_DV), F32)],
        compiler_params=_params("parallel", "parallel", "arbitrary"),
        name="gla_prompt",
    )(proj, proj, proj, proj, small, wup, bg, gn)


def _gladec_kernel(q_ref, k_ref, v_ref, r_ref, sm_ref, wup_ref, bg_ref, gn_ref, s_ref, o_ref, so_ref):
    gg = sm_ref[:, FOX_HEADS:FOX_HEADS + GLA_GATE_RANK]
    la = _gla_gate(gg, wup_ref[...], bg_ref[...])
    eye = (_iota((GLA_DK, GLA_DK), 0) == _iota((GLA_DK, GLA_DK), 1)).astype(F32)
    for h in range(GLA_HEADS):
        ks = slice(h * GLA_DK, (h + 1) * GLA_DK)
        vs = slice(h * GLA_DV, (h + 1) * GLA_DV)
        rows3 = jnp.concatenate([jnp.exp(la[:, ks]), k_ref[:, ks], q_ref[:, ks] * (GLA_DK ** -0.5),
                                 jnp.zeros((SUBLANES - 3, GLA_DK), F32)], axis=0)
        cols = lax.dot_general(eye, rows3, NT_DIMS, precision=HI, preferred_element_type=F32)
        s_new = cols[:, 0:1] * s_ref[h] + cols[:, 1:2] * v_ref[:, vs]
        so_ref[h] = s_new
        o = jnp.sum(cols[:, 2:3] * s_new, axis=0, keepdims=True)
        o_ref[:, vs] = _gla_out(o, gn_ref[...], r_ref[:, vs]).astype(o_ref.dtype)


def _gla_decode(gq, gk, gv, gr, small3, wup, bg, gn, state):
    db = gq.shape[0]
    st =pl.BlockSpec((None, GLA_HEADS, GLA_DK, GLA_DV), lambda i: (i, 0, 0, 0))
    return pl.pallas_call(
        _gladec_kernel,
        grid=(db,),
        in_specs=[pl.BlockSpec((None, 1, GLA_QK_WIDTH), lambda i: (i, 0, 0)),
                  pl.BlockSpec((None, 1, GLA_QK_WIDTH), lambda i: (i, 0, 0)),
                  pl.BlockSpec((None, 1, GLA_WIDTH), lambda i: (i, 0, 0)),
                  pl.BlockSpec((None, 1, GLA_WIDTH), lambda i: (i, 0, 0)),
                  pl.BlockSpec((None, 1, SMALL_COLS), lambda i: (i, 0, 0)),
                  pl.BlockSpec((GLA_GATE_RANK, GLA_QK_WIDTH), lambda i: (0, 0)),
                  pl.BlockSpec((1, GLA_QK_WIDTH), lambda i: (0, 0)),
                  pl.BlockSpec((1, GLA_DV), lambda i: (0, 0)),
                  st],
        out_specs=[pl.BlockSpec((None, 1, GLA_WIDTH), lambda i: (i, 0, 0)), st],
        out_shape=[jax.ShapeDtypeStruct((db, 1, GLA_WIDTH), BF16),
                   jax.ShapeDtypeStruct(state.shape, F32)],
        compiler_params=_params("parallel"),
        name="gla_decode",
    )(gq, gk, gv, gr, small3, wup, bg, gn, state)


def _outproj_kernel(a_ref, b_ref, wa_ref, wb_ref, x_ref, g_ref, gt_ref, o_ref, acc_ref):
    j = pl.program_id(1)
    nj = pl.num_programs(1)
    acc_ref[j] = (jnp.dot(a_ref[...], wa_ref[...], preferred_element_type=F32)
                  + jnp.dot(b_ref[...], wb_ref[...], preferred_element_type=F32))

    @pl.when(j == nj - 1)
    def _():
        n_chunks, _, tn = acc_ref.shape
        ss = jnp.zeros((acc_ref.shape[1], 1), F32)
        for c in range(n_chunks):
            z = acc_ref[c]
            ss = ss + jnp.sum(z * z, axis=1, keepdims=True)
        rs = lax.rsqrt(ss / (n_chunks * tn) + NORM_EPS)
        for c in range(n_chunks):
            cs = slice(c * tn, (c + 1) * tn)
            o_ref[:, cs] = x_ref[:, cs] + gt_ref[:, cs] * (acc_ref[c] * rs * g_ref[:, cs])


def _out_proj(o_fox, o_gla, w_out, x, g, mod, per_row, rows_per_seq, tm, tn=512):
    t, d = x.shape
    ka = o_fox.shape[1]
    tn = min(tn, d)
    tps = max(rows_per_seq // tm, 1)
    return pl.pallas_call(
        _outproj_kernel,
        grid=(t // tm, d // tn),
        in_specs=[pl.BlockSpec((tm, ka), lambda i, j: (i, 0)),
                  pl.BlockSpec((tm, ka), lambda i, j: (i, 0)),
                  pl.BlockSpec((ka, tn), lambda i, j: (0, j)),
                  pl.BlockSpec((ka, tn), lambda i, j: (1, j)),
                  pl.BlockSpec((tm, d), lambda i, j: (i, 0)),
                  pl.BlockSpec((1, d), lambda i, j: (0, 0)),
                  _mod_spec(per_row, 2, tm, d, tps)],
        out_specs=pl.BlockSpec((tm, d), lambda i, j: (i, 0)),
        out_shape=jax.ShapeDtypeStruct((t, d), F32),
        scratch_shapes=[pltpu.VMEM((d // tn, tm, tn), F32)],
        compiler_params=_params("parallel", "arbitrary"),
        name="out_proj",
    )(o_fox, o_gla, w_out, w_out, x, g, mod)


def _ffnpre_kernel(x_ref, g_ref, sh_ref, sc_ref, wr_ref, t_ref, lg_ref):
    h = _rms(x_ref[...], g_ref[...]) * (1.0 + sc_ref[...]) + sh_ref[...]
    t_ref[...] = h.astype(BF16)
    lg_ref[...] = jnp.dot(h, wr_ref[...], precision=HI, preferred_element_type=F32)


def _ffn_pre(x, g, mod, per_row, rows_per_seq, w_router, tm):
    t, d = x.shape
    tps = max(rows_per_seq // tm, 1)
    return pl.pallas_call(
        _ffnpre_kernel,
        grid=(t // tm,),
        in_specs=[pl.BlockSpec((tm, d), lambda i: (i, 0)),
                  pl.BlockSpec((1, d), lambda i: (0, 0)),
                  _mod_spec(per_row, 3, tm, d, tps),
                  _mod_spec(per_row, 4, tm, d, tps),
                  pl.BlockSpec((d, N_EXPERTS), lambda i: (0, 0))],
        out_specs=[pl.BlockSpec((tm, d), lambda i: (i, 0)),
                   pl.BlockSpec((tm, N_EXPERTS), lambda i: (i, 0))],
        out_shape=[jax.ShapeDtypeStruct((t, d), BF16),
                   jax.ShapeDtypeStruct((t, N_EXPERTS), F32)],
        compiler_params=_params("parallel"),
        name="ffn_pre",
    )(x, g, mod, mod, w_router)


def _new_expert(be_ref, i):
    return jnp.logical_or(i == 0, be_ref[i] != be_ref[jnp.maximum(i - 1, 0)])


def _gateup_kernel(be_ref, nu_ref, x_ref, wg_ref, wu_ref, h_ref, wgb_ref, wub_ref):
    i = pl.program_id(1)

    @pl.when(jnp.logical_and(i < nu_ref[0], _new_expert(be_ref, i)))
    def _():
        wgb_ref[...] = wg_ref[...].astype(BF16)
        wub_ref[...] = wu_ref[...].astype(BF16)

    @pl.when(i < nu_ref[0])
    def _():
        x = x_ref[...]
        g = jnp.dot(x, wgb_ref[...], preferred_element_type=F32)
        u = jnp.dot(x, wub_ref[...], preferred_element_type=F32)
        h_ref[...] = (_silu(g) * u).astype(h_ref.dtype)


def _expert_gate_up(block_expert, n_used, x_sorted, w_gate, w_up, tm, tf=512):
    r, d = x_sorted.shape
    ff = w_gate.shape[-1]
    wspec = pl.BlockSpec((None, None, d, tf), lambda f, i, be, nu: (0, be[i], 0, f))
    grid_spec = pltpu.PrefetchScalarGridSpec(
        num_scalar_prefetch=2, grid=(ff // tf, r // tm),
        in_specs=[pl.BlockSpec((tm, d), lambda f, i, be, nu: (i, 0)), wspec, wspec],
        out_specs=pl.BlockSpec((tm, tf), lambda f, i, be, nu: (i, f)),
        scratch_shapes=[pltpu.VMEM((d, tf), BF16), pltpu.VMEM((d, tf), BF16)])
    return pl.pallas_call(
        _gateup_kernel, grid_spec=grid_spec,
        out_shape=jax.ShapeDtypeStruct((r, ff), BF16),
        compiler_params=_params("arbitrary", "arbitrary"),
        name="expert_gate_up",
    )(block_expert, n_used, x_sorted, w_gate, w_up)


def _down_kernel(be_ref, nu_ref, h_ref, sw_ref, wd_ref, y_ref, wdb_ref):
    i = pl.program_id(1)

    @pl.when(jnp.logical_and(i < nu_ref[0], _new_expert(be_ref, i)))
    def _():
        wdb_ref[...] = wd_ref[...].astype(BF16)

    @pl.when(i < nu_ref[0])
    def _():
        y = jnp.dot(h_ref[...], wdb_ref[...], preferred_element_type=F32)
        y_ref[...] = (y * sw_ref[...]).astype(y_ref.dtype)


def _expert_down(block_expert, n_used, h_sorted, slot_w, w_down, tm, tn=1024):
    r, ff = h_sorted.shape
    d = w_down.shape[-1]
    tn = min(tn, d)
    grid_spec = pltpu.PrefetchScalarGridSpec(
        num_scalar_prefetch=2, grid=(d // tn, r // tm),
        in_specs=[pl.BlockSpec((tm, ff), lambda n, i, be, nu: (i, 0)),
                  pl.BlockSpec((tm, 1), lambda n, i, be, nu: (i, 0)),
                  pl.BlockSpec((None, None, ff, tn), lambda n, i, be, nu: (0, be[i], 0, n))],
        out_specs=pl.BlockSpec((tm, tn), lambda n, i, be, nu: (i, n)),
        scratch_shapes=[pltpu.VMEM((ff, tn), BF16)])
    return pl.pallas_call(
        _down_kernel, grid_spec=grid_spec,
        out_shape=jax.ShapeDtypeStruct((r, d), BF16),
        compiler_params=_params("arbitrary", "arbitrary"),
        name="expert_down",
    )(block_expert, n_used, h_sorted, slot_w, w_down)


def _shared_kernel(t_ref, wg_ref, wu_ref, wd_ref, rt_ref, x_ref, g_ref, gt_ref, o_ref, acc_ref):
    f = pl.program_id(1)

    @pl.when(f == 0)
    def _():
        acc_ref[...] = rt_ref[...]

    t = t_ref[...]
    h = _silu(jnp.dot(t, wg_ref[...], preferred_element_type=F32)) * jnp.dot(t, wu_ref[...], preferred_element_type=F32)
    acc_ref[...] += jnp.dot(h.astype(BF16), wd_ref[...], preferred_element_type=F32)

    @pl.when(f == pl.num_programs(1) - 1)
    def _():
        o_ref[...] = x_ref[...] + gt_ref[...] * _rms(acc_ref[...], g_ref[...])


def _shared_final(t, wg, wu, wd, routed, x, g, mod, per_row, rows_per_seq, tm, tf=256):
    n, d = x.shape
    ff = wg.shape[1]
    tps = max(rows_per_seq // tm, 1)
    return pl.pallas_call(
        _shared_kernel,
        grid=(n // tm, ff // tf),
        in_specs=[pl.BlockSpec((tm, d), lambda i, f: (i, 0)),
                  pl.BlockSpec((d, tf), lambda i, f: (0, f)),
                  pl.BlockSpec((d, tf), lambda i, f: (0, f)),
                  pl.BlockSpec((tf, d), lambda i, f: (f, 0)),
                  pl.BlockSpec((tm, d), lambda i, f: (i, 0)),
                  pl.BlockSpec((tm, d), lambda i, f: (i, 0)),
                  pl.BlockSpec((1, d), lambda i, f: (0, 0)),
                  _mod_spec(per_row, 5, tm, d, tps)],
        out_specs=pl.BlockSpec((tm, d), lambda i, f: (i, 0)),
        out_shape=jax.ShapeDtypeStruct((n, d), F32),
        scratch_shapes=[pltpu.VMEM((tm, d), F32)],
        compiler_params=_params("parallel", "arbitrary"),
        name="shared_final",
    )(t, wg, wu, wd, routed, x, g, mod)


def _first_max(x, lane, n):
    m = jnp.max(x, axis=1, keepdims=True)
    return m, jnp.min(jnp.where(x == m, lane, n), axis=1, keepdims=True)


def _route_kernel(lg_ref, br_ref, idx_ref, w_ref, rank_ref, cnt_ref, carry_ref):
    @pl.when(pl.program_id(0) == 0)
    def _():
        carry_ref[...] = jnp.zeros_like(carry_ref)

    tm = lg_ref.shape[0]
    scores = _sigmoid(lg_ref[...])
    sel = scores + br_ref[...]
    lane = _iota((tm, N_EXPERTS), 1).astype(F32)
    grp = (_iota((tm, N_EXPERTS), 1) // EXPERTS_PER_GROUP).astype(F32)
    neg = -jnp.inf

    gsc = jnp.full((tm, N_EXPERTS), neg, F32)
    for g in range(N_GROUPS):
        mg = jnp.where(grp == g, sel, neg)
        m1, i1 = _first_max(mg, lane, N_EXPERTS)
        m2 = jnp.max(jnp.where(lane == i1, neg, mg), axis=1, keepdims=True)
        gsc = jnp.where(lane == g, m1 + m2, gsc)
    cand = jnp.full((tm, N_EXPERTS), neg, F32)
    for _ in range(TOPK_GROUPS):
        _, gi = _first_max(gsc, lane, N_EXPERTS)
        cand = jnp.where(grp == gi, sel, cand)
        gsc = jnp.where(lane == gi, neg, gsc)

    col = _iota((tm, TOP_K), 1)
    idx = jnp.zeros((tm, TOP_K), F32)
    wts = jnp.zeros((tm, TOP_K), F32)
    chosen = jnp.zeros((tm, N_EXPERTS), F32)
    hits = []
    for k in range(TOP_K):
        _, ik = _first_max(cand, lane, N_EXPERTS)
        hit = lane == ik
        hits.append(hit)
        cand = jnp.where(hit, neg, cand)
        chosen = jnp.where(hit, 1.0, chosen)
        idx = jnp.where(col == k, ik, idx)
        wts = jnp.where(col == k, jnp.sum(jnp.where(hit, scores, 0.0), axis=1, keepdims=True), wts)
    idx_ref[...] = idx.astype(jnp.int32)
    w_ref[...] = wts / jnp.sum(wts, axis=1, keepdims=True) * ROUTED_SCALE

    below = (_iota((tm, tm), 1) < _iota((tm, tm), 0)).astype(BF16)
    prefix = jnp.dot(below, chosen.astype(BF16), preferred_element_type=F32) + carry_ref[...]
    rank = jnp.zeros((tm, TOP_K), F32)
    for k in range(TOP_K):
        rank = jnp.where(col == k, jnp.sum(jnp.where(hits[k], prefix, 0.0), axis=1, keepdims=True), rank)
    rank_ref[...] = rank.astype(jnp.int32)
    carry_ref[...] += jnp.sum(chosen, axis=0, keepdims=True)
    cnt_ref[...] = carry_ref[...]


def _route(logits, b_router):
    t = logits.shape[0]
    tm = max(m for m in range(SUBLANES, 1025, SUBLANES) if t % m == 0)
    kspec = pl.BlockSpec((tm, TOP_K), lambda i: (i, 0))
    return pl.pallas_call(
        _route_kernel,
        grid=(t // tm,),
        in_specs=[pl.BlockSpec((tm, N_EXPERTS), lambda i: (i, 0)),
                  pl.BlockSpec((1, N_EXPERTS), lambda i: (0, 0))],
        out_specs=[kspec, kspec, kspec, pl.BlockSpec((1, N_EXPERTS), lambda i: (0, 0))],
        out_shape=[jax.ShapeDtypeStruct((t, TOP_K), jnp.int32),
                   jax.ShapeDtypeStruct((t, TOP_K), F32),
                   jax.ShapeDtypeStruct((t, TOP_K), jnp.int32),
                   jax.ShapeDtypeStruct((1, N_EXPERTS), F32)],
        scratch_shapes=[pltpu.VMEM((1, N_EXPERTS), F32)],
        compiler_params=_params("arbitrary"),
        name="route",
    )(logits, b_router.reshape(1, N_EXPERTS))


def _dispatch(idx, wts, rank, counts, tm):
    t = idx.shape[0]
    m = t * TOP_K
    counts = counts.reshape(N_EXPERTS).astype(jnp.int32)
    padded = (counts + tm - 1) // tm * tm
    pad_end = jnp.cumsum(padded)
    pad_start = pad_end - padded
    dest = pad_start[idx] + rank
    n_blocks = -(-(m + N_EXPERTS * (tm - 1)) // tm)
    pairs = jnp.stack([jnp.repeat(jnp.arange(t, dtype=jnp.int32), TOP_K),
                       lax.bitcast_convert_type(wts.reshape(-1), jnp.int32)], axis=-1)
    slots = jnp.zeros((n_blocks * tm, 2), jnp.int32).at[dest.reshape(-1)].set(pairs)
    slot_tok = slots[:, 0]
    slot_w = lax.bitcast_convert_type(slots[:, 1], F32).reshape(-1, 1)
    starts = jnp.arange(n_blocks, dtype=jnp.int32) * tm
    block_expert = jnp.minimum(jnp.sum((pad_end[None, :] <= starts[:, None]).astype(jnp.int32), axis=1), N_EXPERTS - 1)
    n_used = (pad_end[-1] // tm).astype(jnp.int32).reshape(1)
    return dest, slot_tok, slot_w, block_expert, n_used


def _tile(n, pref):
    return pref if n % pref == 0 else n


def kernel(x_prompt, x_sample, c_prompt, c_sample, cache_k, cache_v, cache_logf, state_gla, page_table, w_ada, b_ada, g_attn_pre, g_attn_post, g_ffn_pre, g_ffn_post, w_in, b_forget, w_gla_gate_up, b_gla_gate, g_gla_out, w_out, w_router, b_router, w_e_gate, w_e_up, w_e_down, w_s_gate, w_s_up, w_s_down):
    bsz, seq, d = x_prompt.shape
    db = x_sample.shape[0]
    depth = w_ada.shape[0]
    n_pool, page = cache_k.shape[1], cache_k.shape[2]
    tp, ts = bsz * seq, db
    y_p = x_prompt.reshape(tp, d)
    y_s = x_sample.reshape(ts, d)
    outs = [[] for _ in range(8)]

    n_mod = bsz + db
    mp = -(-n_mod // SUBLANES) * SUBLANES
    c_all = jnp.concatenate([c_prompt, c_sample, jnp.zeros((mp - n_mod, d), F32)], axis=0)

    o_fg = FOX_WIDTH + 2 * FOX_KV_WIDTH
    o_gq = o_fg + FOX_HEADS
    o_gg = o_gq + 2 * GLA_QK_WIDTH + GLA_WIDTH
    o_gr = o_gg + GLA_GATE_RANK

    for l in range(depth):
        mod = _modulation(c_all, w_ada[l], b_ada[l])
        mod_p = mod[:bsz].reshape(bsz, 6, 1, d)
        mod_s = mod[bsz:n_mod].reshape(db, 6, d).transpose(1, 0, 2)

        wl = w_in[l]
        w_big = jnp.concatenate([wl[:, :o_fg], wl[:, o_gq:o_gg], wl[:, o_gr:]], axis=1).astype(BF16)
        w_small = jnp.concatenate([wl[:, o_fg:o_gq], wl[:, o_gg:o_gr],
                                   jnp.zeros((d, SMALL_COLS - FOX_HEADS - GLA_GATE_RANK), F32)], axis=1).astype(BF16)
        b_small = jnp.concatenate([b_forget[l], jnp.zeros((SMALL_COLS - FOX_HEADS,), F32)]).reshape(1, SMALL_COLS)
        w_out_b = w_out[l].astype(BF16)
        wup = w_gla_gate_up[l]
        bg = b_gla_gate[l].reshape(1, GLA_QK_WIDTH)
        gn = g_gla_out[l].reshape(1, GLA_DV)
        g_ap = g_attn_pre[l].reshape(1, d)
        g_ao = g_attn_post[l].reshape(1, d)
        g_fp = g_ffn_pre[l].reshape(1, d)
        g_fo = g_ffn_post[l].reshape(1, d)

        tm_p = _tile(seq, 512)
        proj, small = _in_proj(y_p, g_ap, mod_p, False, seq, w_big, w_small, b_small, tm_p)
        proj3 = proj.reshape(bsz, seq, BIG_COLS)
        small3 = small.reshape(bsz, seq, SMALL_COLS)
        cum = _cum_logf(small3, _tile(seq, 512))
        o_fox = _fox_prompt(proj3, cum.reshape(bsz, FOX_KV_HEADS, FOX_GROUP, seq), _tile(seq, 512))
        o_gla, s_fin = _gla_prompt(proj3, small3, wup, bg, gn)
        outs[0].append(proj3[:, :, COL_FK:COL_FV].reshape(bsz, seq, FOX_KV_HEADS, FOX_HEAD_DIM))
        outs[1].append(proj3[:, :, COL_FV:COL_GQ].reshape(bsz, seq, FOX_KV_HEADS, FOX_HEAD_DIM))
        outs[2].append(small3[:, :, :FOX_HEADS])
        outs[3].append(s_fin)
        y_p = _out_proj(o_fox.reshape(tp, FOX_WIDTH), o_gla.reshape(tp, GLA_WIDTH), w_out_b, y_p, g_ao,
                        mod_p, False, seq, _tile(seq, 256))

        proj_s, small_s = _in_proj(y_s, g_ap, mod_s, True, 1, w_big, w_small, b_small, ts)
        k_new = proj_s[:, COL_FK:COL_FV].reshape(db, FOX_KV_HEADS, FOX_HEAD_DIM)
        v_new = proj_s[:, COL_FV:COL_GQ].reshape(db, FOX_KV_HEADS, FOX_HEAD_DIM)
        lf_new = small_s[:, :FOX_HEADS]
        o_fox_s = _fox_decode(proj_s[:, :FOX_WIDTH].reshape(db, FOX_HEADS, FOX_HEAD_DIM),
                              jnp.repeat(k_new, FOX_GROUP, axis=1), jnp.repeat(v_new, FOX_GROUP, axis=1),
                              lf_new.reshape(db, FOX_HEADS, 1),
                              cache_k[l].reshape(n_pool, page * FOX_KV_HEADS, FOX_HEAD_DIM),
                              cache_v[l].reshape(n_pool, page * FOX_KV_HEADS, FOX_HEAD_DIM),
                              cache_logf[l].transpose(0, 2, 1), page_table)
        o_gla_s, s_new = _gla_decode(proj_s[:, COL_GQ:COL_GK].reshape(db, 1, GLA_QK_WIDTH),
                                     proj_s[:, COL_GK:COL_GV].reshape(db, 1, GLA_QK_WIDTH),
                                     proj_s[:, COL_GV:COL_GR].reshape(db, 1, GLA_WIDTH),
                                     proj_s[:, COL_GR:].reshape(db, 1, GLA_WIDTH),
                                     small_s.reshape(db, 1, SMALL_COLS), wup, bg, gn, state_gla[l])
        outs[4].append(k_new.reshape(db, 1, FOX_KV_HEADS, FOX_HEAD_DIM))
        outs[5].append(v_new.reshape(db, 1, FOX_KV_HEADS, FOX_HEAD_DIM))
        outs[6].append(lf_new.reshape(db, 1, FOX_HEADS))
        outs[7].append(s_new)
        y_s = _out_proj(o_fox_s.reshape(ts, FOX_WIDTH), o_gla_s.reshape(ts, GLA_WIDTH), w_out_b, y_s, g_ao,
                        mod_s, True, 1, ts)

        t_p, lg_p = _ffn_pre(y_p, g_fp, mod_p, False, seq, w_router[l], _tile(seq, 512))
        t_s, lg_s = _ffn_pre(y_s, g_fp, mod_s, True, 1, w_router[l], ts)
        t_all = jnp.concatenate([t_p, t_s], axis=0)
        idx, wts, rank, counts = _route(jnp.concatenate([lg_p, lg_s], axis=0), b_router[l])
        tm_e = 256
        dest, slot_tok, slot_w, block_expert, n_used = _dispatch(idx, wts, rank, counts, tm_e)
        x_sorted = jnp.take(t_all, slot_tok, axis=0)
        h_sorted = _expert_gate_up(block_expert, n_used, x_sorted, w_e_gate[l:l + 1], w_e_up[l:l + 1], tm_e)
        y_sorted = _expert_down(block_expert, n_used, h_sorted, slot_w, w_e_down[l:l + 1], tm_e)
        routed = jnp.sum(jnp.take(y_sorted, dest, axis=0).astype(F32), axis=1)
        wsg, wsu, wsd = w_s_gate[l].astype(BF16), w_s_up[l].astype(BF16), w_s_down[l].astype(BF16)
        y_p = _shared_final(t_p, wsg, wsu, wsd, routed[:tp], y_p, g_fo, mod_p, False, seq, _tile(seq, 256))
        y_s = _shared_final(t_s, wsg, wsu, wsd, routed[tp:], y_s, g_fo, mod_s, True, 1, ts)

    k_p, v_p, f_p, s_p, k_s, v_s, f_s, s_s = [jnp.stack(o) for o in outs]
    return (y_p.reshape(bsz, seq, d), y_s.reshape(db, 1, d), k_p, v_p, f_p, s_p, k_s, v_s, f_s, s_s)
```

```python
import functools

import numpy as np
import jax
import jax.numpy as jnp
from jax import lax
from jax.experimental import pallas as pl
from jax.experimental.pallas import tpu as pltpu

F32 = jnp.float32
BF16 = jnp.bfloat16
HI = lax.Precision.HIGHEST
NT_DIMS = (((1,), (1,)), ((), ()))
TN_DIMS = (((0,), (0,)), ((), ()))

FOX_HEADS = 16
FOX_KV_HEADS = 4
FOX_GROUP = FOX_HEADS // FOX_KV_HEADS
FOX_HEAD_DIM = 128
FOX_WIDTH = FOX_HEADS * FOX_HEAD_DIM
FOX_KV_WIDTH = FOX_KV_HEADS * FOX_HEAD_DIM
GLA_HEADS = 4
GLA_DK = 256
GLA_DV = 512
GLA_QK_WIDTH = GLA_HEADS * GLA_DK
GLA_WIDTH = GLA_HEADS * GLA_DV
GLA_GATE_RANK = 16
GLA_GATE_TEMP = 16.0
GLA_CHUNK = 64
N_EXPERTS = 64
TOP_K = 8
N_GROUPS = 8
TOPK_GROUPS = 4
EXPERTS_PER_GROUP = N_EXPERTS // N_GROUPS
ROUTED_SCALE = 2.5
NORM_EPS = 1e-6

LANES = 128
SUBLANES = 8
VMEM_LIMIT_BYTES = 56 * 1024 * 1024

COL_FQ = 0
COL_FK = COL_FQ + FOX_WIDTH
COL_FV = COL_FK + FOX_KV_WIDTH
COL_GQ = COL_FV + FOX_KV_WIDTH
COL_GK = COL_GQ + GLA_QK_WIDTH
COL_GV = COL_GK + GLA_QK_WIDTH
COL_GR = COL_GV + GLA_WIDTH
BIG_COLS = COL_GR + GLA_WIDTH
SMALL_COLS = LANES


def _params(*sem):
    return pltpu.CompilerParams(dimension_semantics=sem, vmem_limit_bytes=VMEM_LIMIT_BYTES)


def _sigmoid(x):
    return 1.0 / (1.0 + jnp.exp(-x))


def _silu(x):
    return x * _sigmoid(x)


def _log_sigmoid(x):
    return jnp.minimum(x, 0.0) - jnp.log(1.0 + jnp.exp(-jnp.abs(x)))


def _rms(x, g):
    return x * lax.rsqrt(jnp.mean(x * x, axis=-1, keepdims=True) + NORM_EPS) * g


def _iota(shape, dim):
    return lax.broadcasted_iota(jnp.int32, shape, dim)


def _mod_kernel(c_ref, w_ref, b_ref, o_ref):
    a = _silu(c_ref[...]).astype(BF16)
    o_ref[...] = jnp.dot(a, w_ref[...].astype(BF16), preferred_element_type=F32) + b_ref[...]


def _modulation(c_all, w_ada, b_ada, tn=512):
    mp, d = c_all.shape
    n = w_ada.shape[1]
    return pl.pallas_call(
        _mod_kernel,
        grid=(n // tn,),
        in_specs=[pl.BlockSpec((mp, d), lambda j: (0, 0)),
                  pl.BlockSpec((d, tn), lambda j: (0, j)),
                  pl.BlockSpec((1, tn), lambda j: (0, j))],
        out_specs=pl.BlockSpec((mp, tn), lambda j: (0, j)),
        out_shape=jax.ShapeDtypeStruct((mp, n), F32),
        compiler_params=_params("parallel"),
        name="modulation",
    )(c_all, w_ada, b_ada.reshape(1, n))


def _mod_spec(per_row, which, tm, d, tiles_per_seq):
    if per_row:
        return pl.BlockSpec((None, tm, d), lambda i, *_: (which, i, 0))
    return pl.BlockSpec((None, None, 1, d), lambda i, *_: (i // tiles_per_seq, which, 0, 0))


def _inproj_kernel(x_ref, g_ref, sh_ref, sc_ref, w_ref, ws_ref, bs_ref, o_ref, os_ref, h_ref):
    @pl.when(pl.program_id(1) == 0)
    def _():
        h = _rms(x_ref[...], g_ref[...]) * (1.0 + sc_ref[...]) + sh_ref[...]
        hb = h.astype(BF16)
        h_ref[...] = hb
        sm = jnp.dot(hb, ws_ref[...], preferred_element_type=F32) + bs_ref[...]
        os_ref[...] = jnp.where(_iota(sm.shape, 1) < FOX_HEADS, _log_sigmoid(sm), sm)

    o_ref[...] = jnp.dot(h_ref[...], w_ref[...], preferred_element_type=F32)


def _in_proj(x, g, mod, per_row, rows_per_seq, w_big, w_small, b_small, tm, tn=512):
    t, d = x.shape
    tps = max(rows_per_seq // tm, 1)
    return pl.pallas_call(
        _inproj_kernel,
        grid=(t // tm, BIG_COLS // tn),
        in_specs=[pl.BlockSpec((tm, d), lambda i, j: (i, 0)),
                  pl.BlockSpec((1, d), lambda i, j: (0, 0)),
                  _mod_spec(per_row, 0, tm, d, tps),
                  _mod_spec(per_row, 1, tm, d, tps),
                  pl.BlockSpec((d, tn), lambda i, j: (0, j)),
                  pl.BlockSpec((d, SMALL_COLS), lambda i, j: (0, 0)),
                  pl.BlockSpec((1, SMALL_COLS), lambda i, j: (0, 0))],
        out_specs=[pl.BlockSpec((tm, tn), lambda i, j: (i, j)),
                   pl.BlockSpec((tm, SMALL_COLS), lambda i, j: (i, 0))],
        out_shape=[jax.ShapeDtypeStruct((t, BIG_COLS), F32),
                   jax.ShapeDtypeStruct((t, SMALL_COLS), F32)],
        scratch_shapes=[pltpu.VMEM((tm, d), BF16)],
        compiler_params=_params("parallel", "arbitrary"),
        name="in_proj",
    )(x, g, mod, mod, w_big, w_small, b_small)


def _cum_kernel(x_ref, o_ref, carry_ref):
    @pl.when(pl.program_id(1) == 0)
    def _():
        carry_ref[...] = jnp.zeros_like(carry_ref)

    x = x_ref[...]
    tl = x.shape[0]
    tri = (_iota((tl, tl), 1) <= _iota((tl, tl), 0)).astype(F32)
    cum = jnp.dot(tri, x, precision=HI, preferred_element_type=F32) + carry_ref[...]
    carry_ref[...] = cum[tl - 1:tl, :]
    o_ref[...] = cum.T[:FOX_HEADS, :]


def _cum_logf(small, tl):
    b, l, _ = small.shape
    return pl.pallas_call(
        _cum_kernel,
        grid=(b, l // tl),
        in_specs=[pl.BlockSpec((None, tl, SMALL_COLS), lambda i, t: (i, t, 0))],
        out_specs=pl.BlockSpec((None, FOX_HEADS, tl), lambda i, t: (i, 0, t)),
        out_shape=jax.ShapeDtypeStruct((b, FOX_HEADS, l), F32),
        scratch_shapes=[pltpu.VMEM((1, SMALL_COLS), F32)],
        compiler_params=_params("parallel", "arbitrary"),
        name="cum_logf",
    )(small)


def _fox_kernel(qi_tab, kj_tab, q_ref, k_ref, v_ref, ck_ref, o_ref, qs_ref, m_ref, l_ref, acc_ref, *, tq):
    p_id = pl.program_id(2)
    qi = qi_tab[p_id]
    kj = kj_tab[p_id]
    scale = FOX_HEAD_DIM ** -0.5

    @pl.when(kj == 0)
    def _():
        m_ref[...] = jnp.full_like(m_ref, -jnp.inf)
        l_ref[...] = jnp.zeros_like(l_ref)
        acc_ref[...] = jnp.zeros_like(acc_ref)
        for g in range(FOX_GROUP):
            qs_ref[g * tq:(g + 1) * tq, :] = (q_ref[:, g * FOX_HEAD_DIM:(g + 1) * FOX_HEAD_DIM] * scale).astype(BF16)

    def step(masked):
        kb = k_ref[...].astype(BF16)
        vb = v_ref[...].astype(BF16)
        s = lax.dot_general(qs_ref[...], kb, NT_DIMS, preferred_element_type=F32)
        ck = ck_ref[...]
        tk = kb.shape[0]
        parts = []
        for g in range(FOX_GROUP):
            sg = s[g * tq:(g + 1) * tq, :] - ck[g:g + 1, :]
            if masked:
                sg = jnp.where(_iota((tq, tk), 1) <= _iota((tq, tk), 0), sg, -jnp.inf)
            parts.append(sg)
        s = jnp.concatenate(parts, axis=0)
        m_prev = m_ref[...]
        m_next = jnp.maximum(m_prev, jnp.max(s, axis=1, keepdims=True))
        p = jnp.exp(s - m_next[:, :1])
        alpha = jnp.exp(m_prev - m_next)
        l_ref[...] = alpha * l_ref[...] + jnp.sum(p, axis=1, keepdims=True)
        m_ref[...] = m_next
        acc_ref[...] = alpha * acc_ref[...] + jnp.dot(p.astype(BF16), vb, preferred_element_type=F32)

    @pl.when(kj < qi)
    def _():
        step(False)

    @pl.when(kj == qi)
    def _():
        step(True)
        o = acc_ref[...] / l_ref[...]
        for g in range(FOX_GROUP):
            o_ref[:, g * FOX_HEAD_DIM:(g + 1) * FOX_HEAD_DIM] = o[g * tq:(g + 1) * tq, :].astype(o_ref.dtype)


def _fox_prompt(proj, cum4, tq):
    b, l, _ = proj.shape
    nq = l // tq
    pairs = [(i, j) for i in range(nq) for j in range(i + 1)]
    qi_tab = jnp.asarray(np.array([p[0] for p in pairs], np.int32))
    kj_tab = jnp.asarray(np.array([p[1] for p in pairs], np.int32))
    gw = FOX_GROUP * FOX_HEAD_DIM
    kcol = COL_FK // FOX_HEAD_DIM
    vcol = COL_FV // FOX_HEAD_DIM
    grid_spec = pltpu.PrefetchScalarGridSpec(
        num_scalar_prefetch=2,
        grid=(b, FOX_KV_HEADS, len(pairs)),
        in_specs=[pl.BlockSpec((None, tq, gw), lambda i, h, p, qt, kt: (i, qt[p], h)),
                  pl.BlockSpec((None, tq, FOX_HEAD_DIM), lambda i, h, p, qt, kt: (i, kt[p], kcol + h)),
                  pl.BlockSpec((None, tq, FOX_HEAD_DIM), lambda i, h, p, qt, kt: (i, kt[p], vcol + h)),
                  pl.BlockSpec((None, None, FOX_GROUP, tq), lambda i, h, p, qt, kt: (i, h, 0, kt[p]))],
        out_specs=pl.BlockSpec((None, tq, gw), lambda i, h, p, qt, kt: (i, qt[p], h)),
        scratch_shapes=[pltpu.VMEM((FOX_GROUP * tq, FOX_HEAD_DIM), BF16),
                        pltpu.VMEM((FOX_GROUP * tq, FOX_HEAD_DIM), F32),
                        pltpu.VMEM((FOX_GROUP * tq, FOX_HEAD_DIM), F32),
                        pltpu.VMEM((FOX_GROUP * tq, FOX_HEAD_DIM), F32)],
    )
    return pl.pallas_call(
        functools.partial(_fox_kernel, tq=tq),
        grid_spec=grid_spec,
        out_shape=jax.ShapeDtypeStruct((b, l, FOX_WIDTH), BF16),
        compiler_params=_params("parallel", "parallel", "arbitrary"),
        name="fox_prompt",
    )(qi_tab, kj_tab, proj, proj, proj, cum4)


def _foxdec_kernel(pt_ref, q_ref, kn_ref, vn_ref, lfn_ref, *refs, n_pages):
    k_refs = refs[:n_pages]
    v_refs = refs[n_pages:2 * n_pages]
    lf_refs = refs[2 * n_pages:3 * n_pages]
    o_ref = refs[3 * n_pages]
    del pt_ref
    scale = FOX_HEAD_DIM ** -0.5
    cols = k_refs[0].shape[0]
    page = cols // FOX_KV_HEADS
    q = q_ref[...] * scale
    qb = q.astype(BF16)
    valid = (_iota((FOX_HEADS, cols), 1) % FOX_KV_HEADS) == (_iota((FOX_HEADS, cols), 0) // FOX_GROUP)
    spread = (_iota((page, cols), 0) <= _iota((page, cols), 1) // FOX_KV_HEADS).astype(F32)

    off = jnp.zeros((FOX_HEADS, 1), F32)
    logits = []
    for p in range(n_pages):
        lf = lf_refs[p][...]
        cum = jnp.dot(lf, spread, precision=HI, preferred_element_type=F32) + off
        off = off + jnp.sum(lf, axis=1, keepdims=True)
        sp = lax.dot_general(qb, k_refs[p][...].astype(BF16), NT_DIMS, preferred_element_type=F32)
        logits.append(jnp.where(valid, sp - cum, -jnp.inf))
    s_new = jnp.sum(q * kn_ref[...], axis=1, keepdims=True) - (off + lfn_ref[...])

    m = s_new
    for lg in logits:
        m = jnp.maximum(m, jnp.max(lg, axis=1, keepdims=True))
    p_new = jnp.exp(s_new - m)
    l = p_new
    acc = p_new * vn_ref[...]
    for p in range(n_pages):
        pp = jnp.exp(logits[p] - m)
        l = l + jnp.sum(pp, axis=1, keepdims=True)
        acc = acc + jnp.dot(pp.astype(BF16), v_refs[p][...].astype(BF16), preferred_element_type=F32)
    o_ref[...] = (acc / l).astype(o_ref.dtype)


def _fox_decode(q, k_new, v_new, lf_new, cache_k, cache_v, cache_lft, page_table):
    db, n_pages = page_table.shape

    def page_spec(p, shape):
        return pl.BlockSpec((None,) + shape, lambda i, pt: (pt[i, p], 0, 0))

    row = pl.BlockSpec((None, FOX_HEADS, FOX_HEAD_DIM), lambda i, pt: (i, 0, 0))
    in_specs = [row, row, row, pl.BlockSpec((None, FOX_HEADS, 1), lambda i, pt: (i, 0, 0))]
    in_specs += [page_spec(p, cache_k.shape[1:]) for p in range(n_pages)]
    in_specs += [page_spec(p, cache_v.shape[1:]) for p in range(n_pages)]
    in_specs += [page_spec(p, cache_lft.shape[1:]) for p in range(n_pages)]
    grid_spec = pltpu.PrefetchScalarGridSpec(
        num_scalar_prefetch=1, grid=(db,), in_specs=in_specs,
        out_specs=pl.BlockSpec((None, FOX_HEADS, FOX_HEAD_DIM), lambda i, pt: (i, 0, 0)))
    return pl.pallas_call(
        functools.partial(_foxdec_kernel, n_pages=n_pages),
        grid_spec=grid_spec,
        out_shape=jax.ShapeDtypeStruct((db, FOX_HEADS, FOX_HEAD_DIM), BF16),
        compiler_params=_params("parallel"),
        name="fox_decode",
    )(page_table, q, k_new, v_new, lf_new, *([cache_k] * n_pages), *([cache_v] * n_pages),
      *([cache_lft] * n_pages))


def _gla_gate(gg, wup, bg):
    return _log_sigmoid(jnp.dot(gg, wup, precision=HI, preferred_element_type=F32) + bg) / GLA_GATE_TEMP


def _gla_out(o, g, gr):
    return _rms(o, g) * _silu(gr)


def _gla_kernel(q_ref, k_ref, v_ref, r_ref, sm_ref, wup_ref, bg_ref, gn_ref, o_ref, s_ref):
    @pl.when(pl.program_id(2) == 0)
    def _():
        s_ref[...] = jnp.zeros_like(s_ref)

    c = q_ref.shape[0]
    q = q_ref[...] * (GLA_DK ** -0.5)
    k = k_ref[...]
    v = v_ref[...]
    gg = sm_ref[:, FOX_HEADS:FOX_HEADS + GLA_GATE_RANK]
    la = _gla_gate(gg, wup_ref[...], bg_ref[...])
    tri = (_iota((c, c), 1) <= _iota((c, c), 0)).astype(F32)
    b = jnp.dot(tri, la, precision=HI, preferred_element_type=F32)
    b_last = b[c - 1:c, :]
    s_old = s_ref[...]
    vb = v.astype(BF16)
    o = jnp.dot((q * jnp.exp(b)).astype(BF16), s_old.astype(BF16), preferred_element_type=F32)

    rows = _iota((c, GLA_DK), 0)
    lane = _iota((c, c), 1)
    a = jnp.zeros((c, c), F32)
    for j in range(c):
        w = jnp.exp(jnp.where(rows >= j, b - b[j:j + 1, :], -jnp.inf))
        col = jnp.sum(q * w * k[j:j + 1, :], axis=1, keepdims=True)
        a = jnp.where(lane == j, col, a)
    o = o + jnp.dot(a.astype(BF16), vb, preferred_element_type=F32)
    o_ref[...] = _gla_out(o, gn_ref[...], r_ref[...]).astype(o_ref.dtype)

    kd = (k * jnp.exp(b_last - b)).astype(BF16)
    ones = jnp.ones((c, LANES), F32)
    decay = jnp.exp(lax.dot_general(la, ones, TN_DIMS, precision=HI, preferred_element_type=F32))
    decay = jnp.concatenate([decay] * (GLA_DV // LANES), axis=1)
    s_ref[...] = decay * s_old + lax.dot_general(kd, vb, TN_DIMS, preferred_element_type=F32)


def _gla_prompt(proj, small, wup, bg, gn):
    b, l, _ = proj.shape
    c = GLA_CHUNK if l % GLA_CHUNK == 0 else l
    qc, kc = COL_GQ // GLA_DK, COL_GK // GLA_DK
    vc, rc = COL_GV // GLA_DV, COL_GR // GLA_DV
    return pl.pallas_call(
        _gla_kernel,
        grid=(b, GLA_HEADS, l // c),
        in_specs=[pl.BlockSpec((None, c, GLA_DK), lambda i, h, t: (i, t, qc + h)),
                  pl.BlockSpec((None, c, GLA_DK), lambda i, h, t: (i, t, kc + h)),
                  pl.BlockSpec((None, c, GLA_DV), lambda i, h, t: (i, t, vc + h)),
                  pl.BlockSpec((None, c, GLA_DV), lambda i, h, t: (i, t, rc + h)),
                  pl.BlockSpec((None, c, SMALL_COLS), lambda i, h, t: (i, t, 0)),
                  pl.BlockSpec((GLA_GATE_RANK, GLA_DK), lambda i, h, t: (0, h)),
                  pl.BlockSpec((1, GLA_DK), lambda i, h, t: (0, h)),
                  pl.BlockSpec((1, GLA_DV), lambda i, h, t: (0, 0))],
        out_specs=[pl.BlockSpec((None, c, GLA_DV), lambda i, h, t: (i, t, h)),
                   pl.BlockSpec((None, None, GLA_DK, GLA_DV), lambda i, h, t: (i, h, 0, 0))],
        out_shape=[jax.ShapeDtypeStruct((b, l, GLA_WIDTH), BF16),
                   jax.ShapeDtypeStruct((b, GLA_HEADS, GLA_DK, GLA_DV), F32)],
        compiler_params=_params("parallel", "parallel", "arbitrary"),
        name="gla_prompt",
    )(proj, proj, proj, proj, small, wup, bg, gn)


def _gladec_kernel(q_ref, k_ref, v_ref, r_ref, sm_ref, wup_ref, bg_ref, gn_ref, s_ref, o_ref, so_ref):
    gg = sm_ref[:, FOX_HEADS:FOX_HEADS + GLA_GATE_RANK]
    la = _gla_gate(gg, wup_ref[...], bg_ref[...])
    eye = (_iota((GLA_DK, GLA_DK), 0) == _iota((GLA_DK, GLA_DK), 1)).astype(F32)
    for h in range(GLA_HEADS):
        ks = slice(h * GLA_DK, (h + 1) * GLA_DK)
        vs = slice(h * GLA_DV, (h + 1) * GLA_DV)
        rows3 = jnp.concatenate([jnp.exp(la[:, ks]), k_ref[:, ks], q_ref[:, ks] * (GLA_DK ** -0.5),
                                 jnp.zeros((SUBLANES - 3, GLA_DK), F32)], axis=0)
        cols = lax.dot_general(eye, rows3, NT_DIMS, precision=HI, preferred_element_type=F32)
        s_new = cols[:, 0:1] * s_ref[h] + cols[:, 1:2] * v_ref[:, vs]
        so_ref[h] = s_new
        o = jnp.sum(cols[:, 2:3] * s_new, axis=0, keepdims=True)
        o_ref[:, vs] = _gla_out(o, gn_ref[...], r_ref[:, vs]).astype(o_ref.dtype)


def _gla_decode(gq, gk, gv, gr, small3, wup, bg, gn, state):
    db = gq.shape[0]
    st =pl.BlockSpec((None, GLA_HEADS, GLA_DK, GLA_DV), lambda i: (i, 0, 0, 0))
    return pl.pallas_call(
        _gladec_kernel,
        grid=(db,),
        in_specs=[pl.BlockSpec((None, 1, GLA_QK_WIDTH), lambda i: (i, 0, 0)),
                  pl.BlockSpec((None, 1, GLA_QK_WIDTH), lambda i: (i, 0, 0)),
                  pl.BlockSpec((None, 1, GLA_WIDTH), lambda i: (i, 0, 0)),
                  pl.BlockSpec((None, 1, GLA_WIDTH), lambda i: (i, 0, 0)),
                  pl.BlockSpec((None, 1, SMALL_COLS), lambda i: (i, 0, 0)),
                  pl.BlockSpec((GLA_GATE_RANK, GLA_QK_WIDTH), lambda i: (0, 0)),
                  pl.BlockSpec((1, GLA_QK_WIDTH), lambda i: (0, 0)),
                  pl.BlockSpec((1, GLA_DV), lambda i: (0, 0)),
                  st],
        out_specs=[pl.BlockSpec((None, 1, GLA_WIDTH), lambda i: (i, 0, 0)), st],
        out_shape=[jax.ShapeDtypeStruct((db, 1, GLA_WIDTH), BF16),
                   jax.ShapeDtypeStruct(state.shape, F32)],
        compiler_params=_params("parallel"),
        name="gla_decode",
    )(gq, gk, gv, gr, small3, wup, bg, gn, state)


def _outproj_kernel(a_ref, b_ref, wa_ref, wb_ref, x_ref, g_ref, gt_ref, o_ref, acc_ref):
    j = pl.program_id(1)
    nj = pl.num_programs(1)
    acc_ref[j] = (jnp.dot(a_ref[...], wa_ref[...], preferred_element_type=F32)
                  + jnp.dot(b_ref[...], wb_ref[...], preferred_element_type=F32))

    @pl.when(j == nj - 1)
    def _():
        n_chunks, _, tn = acc_ref.shape
        ss = jnp.zeros((acc_ref.shape[1], 1), F32)
        for c in range(n_chunks):
            z = acc_ref[c]
            ss = ss + jnp.sum(z * z, axis=1, keepdims=True)
        rs = lax.rsqrt(ss / (n_chunks * tn) + NORM_EPS)
        for c in range(n_chunks):
            cs = slice(c * tn, (c + 1) * tn)
            o_ref[:, cs] = x_ref[:, cs] + gt_ref[:, cs] * (acc_ref[c] * rs * g_ref[:, cs])


def _out_proj(o_fox, o_gla, w_out, x, g, mod, per_row, rows_per_seq, tm, tn=512):
    t, d = x.shape
    ka = o_fox.shape[1]
    tn = min(tn, d)
    tps = max(rows_per_seq // tm, 1)
    return pl.pallas_call(
        _outproj_kernel,
        grid=(t // tm, d // tn),
        in_specs=[pl.BlockSpec((tm, ka), lambda i, j: (i, 0)),
                  pl.BlockSpec((tm, ka), lambda i, j: (i, 0)),
                  pl.BlockSpec((ka, tn), lambda i, j: (0, j)),
                  pl.BlockSpec((ka, tn), lambda i, j: (1, j)),
                  pl.BlockSpec((tm, d), lambda i, j: (i, 0)),
                  pl.BlockSpec((1, d), lambda i, j: (0, 0)),
                  _mod_spec(per_row, 2, tm, d, tps)],
        out_specs=pl.BlockSpec((tm, d), lambda i, j: (i, 0)),
        out_shape=jax.ShapeDtypeStruct((t, d), F32),
        scratch_shapes=[pltpu.VMEM((d // tn, tm, tn), F32)],
        compiler_params=_params("parallel", "arbitrary"),
        name="out_proj",
    )(o_fox, o_gla, w_out, w_out, x, g, mod)


def _ffnpre_kernel(x_ref, g_ref, sh_ref, sc_ref, wr_ref, t_ref, lg_ref):
    h = _rms(x_ref[...], g_ref[...]) * (1.0 + sc_ref[...]) + sh_ref[...]
    t_ref[...] = h.astype(BF16)
    lg_ref[...] = jnp.dot(h, wr_ref[...], precision=HI, preferred_element_type=F32)


def _ffn_pre(x, g, mod, per_row, rows_per_seq, w_router, tm):
    t, d = x.shape
    tps = max(rows_per_seq // tm, 1)
    return pl.pallas_call(
        _ffnpre_kernel,
        grid=(t // tm,),
        in_specs=[pl.BlockSpec((tm, d), lambda i: (i, 0)),
                  pl.BlockSpec((1, d), lambda i: (0, 0)),
                  _mod_spec(per_row, 3, tm, d, tps),
                  _mod_spec(per_row, 4, tm, d, tps),
                  pl.BlockSpec((d, N_EXPERTS), lambda i: (0, 0))],
        out_specs=[pl.BlockSpec((tm, d), lambda i: (i, 0)),
                   pl.BlockSpec((tm, N_EXPERTS), lambda i: (i, 0))],
        out_shape=[jax.ShapeDtypeStruct((t, d), BF16),
                   jax.ShapeDtypeStruct((t, N_EXPERTS), F32)],
        compiler_params=_params("parallel"),
        name="ffn_pre",
    )(x, g, mod, mod, w_router)


def _new_expert(be_ref, i):
    return jnp.logical_or(i == 0, be_ref[i] != be_ref[jnp.maximum(i - 1, 0)])


def _stream_weights(be_ref, nu_ref, run_ref, rexp_ref, nr_ref, copies, cast):
    p = pl.program_id(0)
    i = pl.program_id(1)

    @pl.when(jnp.logical_and(i < nu_ref[0], _new_expert(be_ref, i)))
    def _():
        run = run_ref[i]
        last = run + 1 == nr_ref[0]

        @pl.when(jnp.logical_and(p == 0, i == 0))
        def _():
            for c in copies(p, be_ref[i]):
                c.start()

        for c in copies(p, be_ref[i]):
            c.wait()
        cast()

        @pl.when(jnp.logical_or(jnp.logical_not(last), p + 1 < pl.num_programs(0)))
        def _():
            for c in copies(jnp.where(last, p + 1, p), rexp_ref[jnp.where(last, 0, run + 1)]):
                c.start()


def _gateup_kernel(be_ref, nu_ref, run_ref, rexp_ref, nr_ref, x_ref, wg_hbm, wu_hbm, h_ref,
                   wgf_ref, wuf_ref, wgb_ref, wub_ref, sem):
    tf = wgf_ref.shape[1]

    def copies(p, e):
        cols = pl.ds(pl.multiple_of(p * tf, tf), tf)
        return (pltpu.make_async_copy(wg_hbm.at[0, e, :, cols], wgf_ref, sem.at[0]),
                pltpu.make_async_copy(wu_hbm.at[0, e, :, cols], wuf_ref, sem.at[1]))

    def cast():
        wgb_ref[...] = wgf_ref[...].astype(BF16)
        wub_ref[...] = wuf_ref[...].astype(BF16)

    _stream_weights(be_ref, nu_ref, run_ref, rexp_ref, nr_ref, copies, cast)

    @pl.when(pl.program_id(1) < nu_ref[0])
    def _():
        x = x_ref[...]
        g = jnp.dot(x, wgb_ref[...], preferred_element_type=F32)
        u = jnp.dot(x, wub_ref[...], preferred_element_type=F32)
        h_ref[...] = (_silu(g) * u).astype(h_ref.dtype)

    @pl.when(pl.program_id(1) >= nu_ref[0])
    def _():
        h_ref[...] = jnp.zeros_like(h_ref)


def _expert_gate_up(tables, x_sorted, w_gate, w_up, tm, tf):
    r, d = x_sorted.shape
    ff = w_gate.shape[-1]
    tf = min(tf, ff)
    grid_spec = pltpu.PrefetchScalarGridSpec(
        num_scalar_prefetch=len(tables), grid=(ff // tf, r // tm),
        in_specs=[pl.BlockSpec((tm, d), lambda p, i, *_: (i, 0)),
                  pl.BlockSpec(memory_space=pl.ANY), pl.BlockSpec(memory_space=pl.ANY)],
        out_specs=pl.BlockSpec((tm, tf), lambda p, i, *_: (i, p)),
        scratch_shapes=[pltpu.VMEM((d, tf), F32), pltpu.VMEM((d, tf), F32),
                        pltpu.VMEM((d, tf), BF16), pltpu.VMEM((d, tf), BF16),
                        pltpu.SemaphoreType.DMA((2,))])
    return pl.pallas_call(
        _gateup_kernel, grid_spec=grid_spec,
        out_shape=jax.ShapeDtypeStruct((r, ff), BF16),
        compiler_params=_params("arbitrary", "arbitrary"),
        name="expert_gate_up",
    )(*tables, x_sorted, w_gate, w_up)


def _down_kernel(be_ref, nu_ref, run_ref, rexp_ref, nr_ref, h_ref, sw_ref, wd_hbm, y_ref, wdf_ref, wdb_ref, sem):
    tn = wdf_ref.shape[1]

    def copies(p, e):
        cols = pl.ds(pl.multiple_of(p * tn, tn), tn)
        return (pltpu.make_async_copy(wd_hbm.at[0, e, :, cols], wdf_ref, sem.at[0]),)

    def cast():
        wdb_ref[...] = wdf_ref[...].astype(BF16)

    _stream_weights(be_ref, nu_ref, run_ref, rexp_ref, nr_ref, copies, cast)

    @pl.when(pl.program_id(1) < nu_ref[0])
    def _():
        y = jnp.dot(h_ref[...], wdb_ref[...], preferred_element_type=F32)
        y_ref[...] = (y * sw_ref[...]).astype(y_ref.dtype)

    @pl.when(pl.program_id(1) >= nu_ref[0])
    def _():
        y_ref[...] = jnp.zeros_like(y_ref)


def _expert_down(tables, h_sorted, slot_w, w_down, tm, tn):
    r, ff = h_sorted.shape
    d = w_down.shape[-1]
    tn = min(tn, d)
    grid_spec = pltpu.PrefetchScalarGridSpec(
        num_scalar_prefetch=len(tables), grid=(d // tn, r // tm),
        in_specs=[pl.BlockSpec((tm, ff), lambda p, i, *_: (i, 0)),
                  pl.BlockSpec((tm, 1), lambda p, i, *_: (i, 0)),
                  pl.BlockSpec(memory_space=pl.ANY)],
        out_specs=pl.BlockSpec((tm, tn), lambda p, i, *_: (i, p)),
        scratch_shapes=[pltpu.VMEM((ff, tn), F32), pltpu.VMEM((ff, tn), BF16), pltpu.SemaphoreType.DMA((1,))])
    return pl.pallas_call(
        _down_kernel, grid_spec=grid_spec,
        out_shape=jax.ShapeDtypeStruct((r, d), BF16),
        compiler_params=_params("arbitrary", "arbitrary"),
        name="expert_down",
    )(*tables, h_sorted, slot_w, w_down)


def _shared_kernel(t_ref, wg_ref, wu_ref, wd_ref, rt_ref, x_ref, g_ref, gt_ref, o_ref, acc_ref):
    f = pl.program_id(1)

    @pl.when(f == 0)
    def _():
        acc_ref[...] = rt_ref[...]

    t = t_ref[...]
    h = _silu(jnp.dot(t, wg_ref[...], preferred_element_type=F32)) * jnp.dot(t, wu_ref[...], preferred_element_type=F32)
    acc_ref[...] += jnp.dot(h.astype(BF16), wd_ref[...], preferred_element_type=F32)

    @pl.when(f == pl.num_programs(1) - 1)
    def _():
        o_ref[...] = x_ref[...] + gt_ref[...] * _rms(acc_ref[...], g_ref[...])


def _shared_final(t, wg, wu, wd, routed, x, g, mod, per_row, rows_per_seq, tm, tf=256):
    n, d = x.shape
    ff = wg.shape[1]
    tps = max(rows_per_seq // tm, 1)
    return pl.pallas_call(
        _shared_kernel,
        grid=(n // tm, ff // tf),
        in_specs=[pl.BlockSpec((tm, d), lambda i, f: (i, 0)),
                  pl.BlockSpec((d, tf), lambda i, f: (0, f)),
                  pl.BlockSpec((d, tf), lambda i, f: (0, f)),
                  pl.BlockSpec((tf, d), lambda i, f: (f, 0)),
                  pl.BlockSpec((tm, d), lambda i, f: (i, 0)),
                  pl.BlockSpec((tm, d), lambda i, f: (i, 0)),
                  pl.BlockSpec((1, d), lambda i, f: (0, 0)),
                  _mod_spec(per_row, 5, tm, d, tps)],
        out_specs=pl.BlockSpec((tm, d), lambda i, f: (i, 0)),
        out_shape=jax.ShapeDtypeStruct((n, d), F32),
        scratch_shapes=[pltpu.VMEM((tm, d), F32)],
        compiler_params=_params("parallel", "arbitrary"),
        name="shared_final",
    )(t, wg, wu, wd, routed, x, g, mod)


def _first_max(x, lane, n):
    m = jnp.max(x, axis=1, keepdims=True)
    return m, jnp.min(jnp.where(x == m, lane, n), axis=1, keepdims=True)


def _route_kernel(lg_ref, br_ref, idx_ref, w_ref, rank_ref, cnt_ref, carry_ref):
    @pl.when(pl.program_id(0) == 0)
    def _():
        carry_ref[...] = jnp.zeros_like(carry_ref)

    tm = lg_ref.shape[0]
    scores = _sigmoid(lg_ref[...])
    sel = scores + br_ref[...]
    lane = _iota((tm, N_EXPERTS), 1).astype(F32)
    grp = (_iota((tm, N_EXPERTS), 1) // EXPERTS_PER_GROUP).astype(F32)
    neg = -jnp.inf

    gsc = jnp.full((tm, N_EXPERTS), neg, F32)
    for g in range(N_GROUPS):
        mg = jnp.where(grp == g, sel, neg)
        m1, i1 = _first_max(mg, lane, N_EXPERTS)
        m2 = jnp.max(jnp.where(lane == i1, neg, mg), axis=1, keepdims=True)
        gsc = jnp.where(lane == g, m1 + m2, gsc)
    cand = jnp.full((tm, N_EXPERTS), neg, F32)
    for _ in range(TOPK_GROUPS):
        _, gi = _first_max(gsc, lane, N_EXPERTS)
        cand = jnp.where(grp == gi, sel, cand)
        gsc = jnp.where(lane == gi, neg, gsc)

    col = _iota((tm, TOP_K), 1)
    idx = jnp.zeros((tm, TOP_K), F32)
    wts = jnp.zeros((tm, TOP_K), F32)
    chosen = jnp.zeros((tm, N_EXPERTS), F32)
    hits = []
    for k in range(TOP_K):
        _, ik = _first_max(cand, lane, N_EXPERTS)
        hit = lane == ik
        hits.append(hit)
        cand = jnp.where(hit, neg, cand)
        chosen = jnp.where(hit, 1.0, chosen)
        idx = jnp.where(col == k, ik, idx)
        wts = jnp.where(col == k, jnp.sum(jnp.where(hit, scores, 0.0), axis=1, keepdims=True), wts)
    idx_ref[...] = idx.astype(jnp.int32)
    w_ref[...] = wts / jnp.sum(wts, axis=1, keepdims=True) * ROUTED_SCALE

    below = (_iota((tm, tm), 1) < _iota((tm, tm), 0)).astype(BF16)
    prefix = jnp.dot(below, chosen.astype(BF16), preferred_element_type=F32) + carry_ref[...]
    rank = jnp.zeros((tm, TOP_K), F32)
    for k in range(TOP_K):
        rank = jnp.where(col == k, jnp.sum(jnp.where(hits[k], prefix, 0.0), axis=1, keepdims=True), rank)
    rank_ref[...] = rank.astype(jnp.int32)
    carry_ref[...] += jnp.sum(chosen, axis=0, keepdims=True)
    cnt_ref[...] = carry_ref[...]


def _route(logits, b_router):
    t = logits.shape[0]
    tm = max(m for m in range(SUBLANES, 1025, SUBLANES) if t % m == 0)
    kspec = pl.BlockSpec((tm, TOP_K), lambda i: (i, 0))
    return pl.pallas_call(
        _route_kernel,
        grid=(t // tm,),
        in_specs=[pl.BlockSpec((tm, N_EXPERTS), lambda i: (i, 0)),
                  pl.BlockSpec((1, N_EXPERTS), lambda i: (0, 0))],
        out_specs=[kspec, kspec, kspec, pl.BlockSpec((1, N_EXPERTS), lambda i: (0, 0))],
        out_shape=[jax.ShapeDtypeStruct((t, TOP_K), jnp.int32),
                   jax.ShapeDtypeStruct((t, TOP_K), F32),
                   jax.ShapeDtypeStruct((t, TOP_K), jnp.int32),
                   jax.ShapeDtypeStruct((1, N_EXPERTS), F32)],
        scratch_shapes=[pltpu.VMEM((1, N_EXPERTS), F32)],
        compiler_params=_params("arbitrary"),
        name="route",
    )(logits, b_router.reshape(1, N_EXPERTS))


def _dispatch(idx, wts, rank, counts, tm):
    t = idx.shape[0]
    m = t * TOP_K
    counts = counts.reshape(N_EXPERTS).astype(jnp.int32)
    padded = (counts + tm - 1) // tm * tm
    pad_end = jnp.cumsum(padded)
    pad_start = pad_end - padded
    dest = pad_start[idx] + rank
    n_blocks = -(-(m + N_EXPERTS * (tm - 1)) // tm)
    pairs = jnp.stack([jnp.repeat(jnp.arange(t, dtype=jnp.int32), TOP_K),
                       lax.bitcast_convert_type(wts.reshape(-1), jnp.int32)], axis=-1)
    n_slots = n_blocks * tm
    init = jnp.stack([jnp.arange(n_slots, dtype=jnp.int32) % t, jnp.zeros((n_slots,), jnp.int32)], axis=-1)
    slots = init.at[dest.reshape(-1)].set(pairs, mode="promise_in_bounds", unique_indices=True)
    slot_tok = slots[:, 0]
    slot_w = lax.bitcast_convert_type(slots[:, 1], F32).reshape(-1, 1)
    starts = jnp.arange(n_blocks, dtype=jnp.int32) * tm
    block_expert = jnp.minimum(jnp.sum((pad_end[None, :] <= starts[:, None]).astype(jnp.int32), axis=1), N_EXPERTS - 1)
    n_used = (pad_end[-1] // tm).astype(jnp.int32).reshape(1)
    present = counts > 0
    run_of_expert = jnp.cumsum(present.astype(jnp.int32)) - 1
    run_expert = jnp.argsort(jnp.logical_not(present), stable=True).astype(jnp.int32)
    n_runs = jnp.sum(present.astype(jnp.int32)).reshape(1)
    tables = (block_expert, n_used, run_of_expert[block_expert], run_expert, n_runs)
    return dest, slot_tok, slot_w, tables


def _tile(n, pref):
    return pref if n % pref == 0 else n


def kernel(x_prompt, x_sample, c_prompt, c_sample, cache_k, cache_v, cache_logf, state_gla, page_table, w_ada, b_ada, g_attn_pre, g_attn_post, g_ffn_pre, g_ffn_post, w_in, b_forget, w_gla_gate_up, b_gla_gate, g_gla_out, w_out, w_router, b_router, w_e_gate, w_e_up, w_e_down, w_s_gate, w_s_up, w_s_down):
    bsz, seq, d = x_prompt.shape
    db = x_sample.shape[0]
    depth = w_ada.shape[0]
    n_pool, page = cache_k.shape[1], cache_k.shape[2]
    tp, ts = bsz * seq, db
    y_p = x_prompt.reshape(tp, d)
    y_s = x_sample.reshape(ts, d)
    outs = [[] for _ in range(8)]

    n_mod = bsz + db
    mp = -(-n_mod // SUBLANES) * SUBLANES
    c_all = jnp.concatenate([c_prompt, c_sample, jnp.zeros((mp - n_mod, d), F32)], axis=0)

    o_fg = FOX_WIDTH + 2 * FOX_KV_WIDTH
    o_gq = o_fg + FOX_HEADS
    o_gg = o_gq + 2 * GLA_QK_WIDTH + GLA_WIDTH
    o_gr = o_gg + GLA_GATE_RANK

    for l in range(depth):
        mod = _modulation(c_all, w_ada[l], b_ada[l])
        mod_p = mod[:bsz].reshape(bsz, 6, 1, d)
        mod_s = mod[bsz:n_mod].reshape(db, 6, d).transpose(1, 0, 2)

        wl = w_in[l]
        w_big = jnp.concatenate([wl[:, :o_fg], wl[:, o_gq:o_gg], wl[:, o_gr:]], axis=1).astype(BF16)
        w_small = jnp.concatenate([wl[:, o_fg:o_gq], wl[:, o_gg:o_gr],
                                   jnp.zeros((d, SMALL_COLS - FOX_HEADS - GLA_GATE_RANK), F32)], axis=1).astype(BF16)
        b_small = jnp.concatenate([b_forget[l], jnp.zeros((SMALL_COLS - FOX_HEADS,), F32)]).reshape(1, SMALL_COLS)
        w_out_b = w_out[l].astype(BF16)
        wup = w_gla_gate_up[l]
        bg = b_gla_gate[l].reshape(1, GLA_QK_WIDTH)
        gn = g_gla_out[l].reshape(1, GLA_DV)
        g_ap = g_attn_pre[l].reshape(1, d)
        g_ao = g_attn_post[l].reshape(1, d)
        g_fp = g_ffn_pre[l].reshape(1, d)
        g_fo = g_ffn_post[l].reshape(1, d)

        tm_p = _tile(seq, 512)
        proj, small = _in_proj(y_p, g_ap, mod_p, False, seq, w_big, w_small, b_small, tm_p)
        proj3 = proj.reshape(bsz, seq, BIG_COLS)
        small3 = small.reshape(bsz, seq, SMALL_COLS)
        cum = _cum_logf(small3, _tile(seq, 512))
        o_fox = _fox_prompt(proj3, cum.reshape(bsz, FOX_KV_HEADS, FOX_GROUP, seq), _tile(seq, 512))
        o_gla, s_fin = _gla_prompt(proj3, small3, wup, bg, gn)
        outs[0].append(proj3[:, :, COL_FK:COL_FV].reshape(bsz, seq, FOX_KV_HEADS, FOX_HEAD_DIM))
        outs[1].append(proj3[:, :, COL_FV:COL_GQ].reshape(bsz, seq, FOX_KV_HEADS, FOX_HEAD_DIM))
        outs[2].append(small3[:, :, :FOX_HEADS])
        outs[3].append(s_fin)
        y_p = _out_proj(o_fox.reshape(tp, FOX_WIDTH), o_gla.reshape(tp, GLA_WIDTH), w_out_b, y_p, g_ao,
                        mod_p, False, seq, _tile(seq, 256))

        proj_s, small_s = _in_proj(y_s, g_ap, mod_s, True, 1, w_big, w_small, b_small, ts)
        k_new = proj_s[:, COL_FK:COL_FV].reshape(db, FOX_KV_HEADS, FOX_HEAD_DIM)
        v_new = proj_s[:, COL_FV:COL_GQ].reshape(db, FOX_KV_HEADS, FOX_HEAD_DIM)
        lf_new = small_s[:, :FOX_HEADS]
        o_fox_s = _fox_decode(proj_s[:, :FOX_WIDTH].reshape(db, FOX_HEADS, FOX_HEAD_DIM),
                              jnp.repeat(k_new, FOX_GROUP, axis=1), jnp.repeat(v_new, FOX_GROUP, axis=1),
                              lf_new.reshape(db, FOX_HEADS, 1),
                              cache_k[l].reshape(n_pool, page * FOX_KV_HEADS, FOX_HEAD_DIM),
                              cache_v[l].reshape(n_pool, page * FOX_KV_HEADS, FOX_HEAD_DIM),
                              cache_logf[l].transpose(0, 2, 1), page_table)
        o_gla_s, s_new = _gla_decode(proj_s[:, COL_GQ:COL_GK].reshape(db, 1, GLA_QK_WIDTH),
                                     proj_s[:, COL_GK:COL_GV].reshape(db, 1, GLA_QK_WIDTH),
                                     proj_s[:, COL_GV:COL_GR].reshape(db, 1, GLA_WIDTH),
                                     proj_s[:, COL_GR:].reshape(db, 1, GLA_WIDTH),
                                     small_s.reshape(db, 1, SMALL_COLS), wup, bg, gn, state_gla[l])
        outs[4].append(k_new.reshape(db, 1, FOX_KV_HEADS, FOX_HEAD_DIM))
        outs[5].append(v_new.reshape(db, 1, FOX_KV_HEADS, FOX_HEAD_DIM))
        outs[6].append(lf_new.reshape(db, 1, FOX_HEADS))
        outs[7].append(s_new)
        y_s = _out_proj(o_fox_s.reshape(ts, FOX_WIDTH), o_gla_s.reshape(ts, GLA_WIDTH), w_out_b, y_s, g_ao,
                        mod_s, True, 1, ts)

        t_p, lg_p = _ffn_pre(y_p, g_fp, mod_p, False, seq, w_router[l], _tile(seq, 512))
        t_s, lg_s = _ffn_pre(y_s, g_fp, mod_s, True, 1, w_router[l], ts)
        t_all = jnp.concatenate([t_p, t_s], axis=0)
        idx, wts, rank, counts = _route(jnp.concatenate([lg_p, lg_s], axis=0), b_router[l])
        tm_e = 256
        dest, slot_tok, slot_w, tables = _dispatch(idx, wts, rank, counts, tm_e)
        x_sorted = t_all.at[slot_tok].get(mode="promise_in_bounds")
        h_sorted = _expert_gate_up(tables, x_sorted, w_e_gate[l:l + 1], w_e_up[l:l + 1], tm_e, tf=512)
        y_sorted = _expert_down(tables, h_sorted, slot_w, w_e_down[l:l + 1], tm_e, tn=4096)
        routed = jnp.sum(y_sorted.at[dest.T].get(mode="promise_in_bounds").astype(F32), axis=0)
        wsg, wsu, wsd = w_s_gate[l].astype(BF16), w_s_up[l].astype(BF16), w_s_down[l].astype(BF16)
        y_p = _shared_final(t_p, wsg, wsu, wsd, routed[:tp], y_p, g_fo, mod_p, False, seq, _tile(seq, 256))
        y_s = _shared_final(t_s, wsg, wsu, wsd, routed[tp:], y_s, g_fo, mod_s, True, 1, ts)

    k_p, v_p, f_p, s_p, k_s, v_s, f_s, s_s = [jnp.stack(o) for o in outs]
    return (y_p.reshape(bsz, seq, d), y_s.reshape(db, 1, d), k_p, v_p, f_p, s_p, k_s, v_s, f_s, s_s)
```

```python
import functools
import math

import numpy as np
import jax
import jax.numpy as jnp
from jax import lax
from jax.experimental import pallas as pl
from jax.experimental.pallas import tpu as pltpu

F32 = jnp.float32
BF16 = jnp.bfloat16
HI = lax.Precision.HIGHEST
NT_DIMS = (((1,), (1,)), ((), ()))
TN_DIMS = (((0,), (0,)), ((), ()))

FOX_HEADS = 16
FOX_KV_HEADS = 4
FOX_GROUP = FOX_HEADS // FOX_KV_HEADS
FOX_HEAD_DIM = 128
FOX_WIDTH = FOX_HEADS * FOX_HEAD_DIM
FOX_KV_WIDTH = FOX_KV_HEADS * FOX_HEAD_DIM
GLA_HEADS = 4
GLA_DK = 256
GLA_DV = 512
GLA_QK_WIDTH = GLA_HEADS * GLA_DK
GLA_WIDTH = GLA_HEADS * GLA_DV
GLA_GATE_RANK = 16
GLA_GATE_TEMP = 16.0
GLA_CHUNK = 64
N_EXPERTS = 64
TOP_K = 8
N_GROUPS = 8
TOPK_GROUPS = 4
EXPERTS_PER_GROUP = N_EXPERTS // N_GROUPS
ROUTED_SCALE = 2.5
NORM_EPS = 1e-6

LANES = 128
SUBLANES = 8
VMEM_LIMIT_BYTES = 56 * 1024 * 1024

COL_FQ = 0
COL_FK = COL_FQ + FOX_WIDTH
COL_FV = COL_FK + FOX_KV_WIDTH
COL_GQ = COL_FV + FOX_KV_WIDTH
COL_GK = COL_GQ + GLA_QK_WIDTH
COL_GV = COL_GK + GLA_QK_WIDTH
COL_GR = COL_GV + GLA_WIDTH
BIG_COLS = COL_GR + GLA_WIDTH
SMALL_COLS = LANES


def _params(*sem):
    return pltpu.CompilerParams(dimension_semantics=sem, vmem_limit_bytes=VMEM_LIMIT_BYTES)


def _sigmoid(x):
    return 1.0 / (1.0 + jnp.exp(-x))


def _silu(x):
    return x * _sigmoid(x)


def _log_sigmoid(x):
    return jnp.minimum(x, 0.0) - jnp.log(1.0 + jnp.exp(-jnp.abs(x)))


def _rms(x, g):
    return x * lax.rsqrt(jnp.mean(x * x, axis=-1, keepdims=True) + NORM_EPS) * g


def _iota(shape, dim):
    return lax.broadcasted_iota(jnp.int32, shape, dim)


def _mod_kernel(c_ref, w_ref, b_ref, o_ref):
    a = _silu(c_ref[...]).astype(BF16)
    o_ref[...] = jnp.dot(a, w_ref[...].astype(BF16), preferred_element_type=F32) + b_ref[...]


def _modulation(c_all, w_ada, b_ada, tn=512):
    mp, d = c_all.shape
    n = w_ada.shape[1]
    return pl.pallas_call(
        _mod_kernel,
        grid=(n // tn,),
        in_specs=[pl.BlockSpec((mp, d), lambda j: (0, 0)),
                  pl.BlockSpec((d, tn), lambda j: (0, j)),
                  pl.BlockSpec((1, tn), lambda j: (0, j))],
        out_specs=pl.BlockSpec((mp, tn), lambda j: (0, j)),
        out_shape=jax.ShapeDtypeStruct((mp, n), F32),
        compiler_params=_params("parallel"),
        name="modulation",
    )(c_all, w_ada, b_ada.reshape(1, n))


def _mod_spec(per_row, which, tm, d, tiles_per_seq):
    if per_row:
        return pl.BlockSpec((None, tm, d), lambda i, *_: (which, i, 0))
    return pl.BlockSpec((None, None, 1, d), lambda i, *_: (i // tiles_per_seq, which, 0, 0))


def _inproj_kernel(x_ref, g_ref, sh_ref, sc_ref, w_ref, ws_ref, bs_ref, o_ref, os_ref, h_ref):
    @pl.when(pl.program_id(1) == 0)
    def _():
        h = _rms(x_ref[...], g_ref[...]) * (1.0 + sc_ref[...]) + sh_ref[...]
        hb = h.astype(BF16)
        h_ref[...] = hb
        sm = jnp.dot(hb, ws_ref[...], preferred_element_type=F32) + bs_ref[...]
        os_ref[...] = jnp.where(_iota(sm.shape, 1) < FOX_HEADS, _log_sigmoid(sm), sm)

    o_ref[...] = jnp.dot(h_ref[...], w_ref[...], preferred_element_type=F32)


def _in_proj(x, g, mod, per_row, rows_per_seq, w_big, w_small, b_small, tm, tn=512):
    t, d = x.shape
    tps = max(rows_per_seq // tm, 1)
    return pl.pallas_call(
        _inproj_kernel,
        grid=(t // tm, BIG_COLS // tn),
        in_specs=[pl.BlockSpec((tm, d), lambda i, j: (i, 0), pipeline_mode=pl.Buffered(1)),
                  pl.BlockSpec((1, d), lambda i, j: (0, 0)),
                  _mod_spec(per_row, 0, tm, d, tps),
                  _mod_spec(per_row, 1, tm, d, tps),
                  pl.BlockSpec((d, tn), lambda i, j: (0, j)),
                  pl.BlockSpec((d, SMALL_COLS), lambda i, j: (0, 0)),
                  pl.BlockSpec((1, SMALL_COLS), lambda i, j: (0, 0))],
        out_specs=[pl.BlockSpec((tm, tn), lambda i, j: (i, j)),
                   pl.BlockSpec((tm, SMALL_COLS), lambda i, j: (i, 0))],
        out_shape=[jax.ShapeDtypeStruct((t, BIG_COLS), F32),
                   jax.ShapeDtypeStruct((t, SMALL_COLS), F32)],
        scratch_shapes=[pltpu.VMEM((tm, d), BF16)],
        compiler_params=_params("parallel", "arbitrary"),
        name="in_proj",
    )(x, g, mod, mod, w_big, w_small, b_small)


def _cum_kernel(x_ref, o_ref, carry_ref):
    @pl.when(pl.program_id(1) == 0)
    def _():
        carry_ref[...] = jnp.zeros_like(carry_ref)

    x = x_ref[...]
    tl = x.shape[0]
    tri = (_iota((tl, tl), 1) <= _iota((tl, tl), 0)).astype(F32)
    cum = jnp.dot(tri, x, precision=HI, preferred_element_type=F32) + carry_ref[...]
    carry_ref[...] = cum[tl - 1:tl, :]
    o_ref[...] = cum.T[:FOX_HEADS, :]


def _cum_logf(small, tl):
    b, l, _ = small.shape
    return pl.pallas_call(
        _cum_kernel,
        grid=(b, l // tl),
        in_specs=[pl.BlockSpec((None, tl, SMALL_COLS), lambda i, t: (i, t, 0))],
        out_specs=pl.BlockSpec((None, FOX_HEADS, tl), lambda i, t: (i, 0, t)),
        out_shape=jax.ShapeDtypeStruct((b, FOX_HEADS, l), F32),
        scratch_shapes=[pltpu.VMEM((1, SMALL_COLS), F32)],
        compiler_params=_params("parallel", "arbitrary"),
        name="cum_logf",
    )(small)


def _fox_kernel(qi_tab, kj_tab, q_ref, k_ref, v_ref, ck_ref, o_ref, qs_ref, m_ref, l_ref, acc_ref, *, tq):
    p_id = pl.program_id(2)
    qi = qi_tab[p_id]
    kj = kj_tab[p_id]
    scale = FOX_HEAD_DIM ** -0.5

    @pl.when(kj == 0)
    def _():
        m_ref[...] = jnp.full_like(m_ref, -jnp.inf)
        l_ref[...] = jnp.zeros_like(l_ref)
        acc_ref[...] = jnp.zeros_like(acc_ref)
        for g in range(FOX_GROUP):
            qs_ref[g * tq:(g + 1) * tq, :] = (q_ref[:, g * FOX_HEAD_DIM:(g + 1) * FOX_HEAD_DIM] * scale).astype(BF16)

    def step(masked):
        kb = k_ref[...].astype(BF16)
        vb = v_ref[...].astype(BF16)
        s = lax.dot_general(qs_ref[...], kb, NT_DIMS, preferred_element_type=F32)
        ck = ck_ref[...]
        tk = kb.shape[0]
        parts = []
        for g in range(FOX_GROUP):
            sg = s[g * tq:(g + 1) * tq, :] - ck[g:g + 1, :]
            if masked:
                sg = jnp.where(_iota((tq, tk), 1) <= _iota((tq, tk), 0), sg, -jnp.inf)
            parts.append(sg)
        s = jnp.concatenate(parts, axis=0)
        m_prev = m_ref[...]
        m_next = jnp.maximum(m_prev, jnp.max(s, axis=1, keepdims=True))
        p = jnp.exp(s - m_next[:, :1])
        alpha = jnp.exp(m_prev - m_next)
        l_ref[...] = alpha * l_ref[...] + jnp.sum(p, axis=1, keepdims=True)
        m_ref[...] = m_next
        acc_ref[...] = alpha * acc_ref[...] + jnp.dot(p.astype(BF16), vb, preferred_element_type=F32)

    @pl.when(kj < qi)
    def _():
        step(False)

    @pl.when(kj == qi)
    def _():
        step(True)
        o = acc_ref[...] / l_ref[...]
        for g in range(FOX_GROUP):
            o_ref[:, g * FOX_HEAD_DIM:(g + 1) * FOX_HEAD_DIM] = o[g * tq:(g + 1) * tq, :].astype(o_ref.dtype)


def _fox_prompt(proj, cum4, tq):
    b, l, _ = proj.shape
    nq = l // tq
    pairs = [(i, j) for i in range(nq) for j in range(i + 1)]
    qi_tab = jnp.asarray(np.array([p[0] for p in pairs], np.int32))
    kj_tab = jnp.asarray(np.array([p[1] for p in pairs], np.int32))
    gw = FOX_GROUP * FOX_HEAD_DIM
    kcol = COL_FK // FOX_HEAD_DIM
    vcol = COL_FV // FOX_HEAD_DIM
    grid_spec = pltpu.PrefetchScalarGridSpec(
        num_scalar_prefetch=2,
        grid=(b, FOX_KV_HEADS, len(pairs)),
        in_specs=[pl.BlockSpec((None, tq, gw), lambda i, h, p, qt, kt: (i, qt[p], h)),
                  pl.BlockSpec((None, tq, FOX_HEAD_DIM), lambda i, h, p, qt, kt: (i, kt[p], kcol + h)),
                  pl.BlockSpec((None, tq, FOX_HEAD_DIM), lambda i, h, p, qt, kt: (i, kt[p], vcol + h)),
                  pl.BlockSpec((None, None, FOX_GROUP, tq), lambda i, h, p, qt, kt: (i, h, 0, kt[p]))],
        out_specs=pl.BlockSpec((None, tq, gw), lambda i, h, p, qt, kt: (i, qt[p], h)),
        scratch_shapes=[pltpu.VMEM((FOX_GROUP * tq, FOX_HEAD_DIM), BF16),
                        pltpu.VMEM((FOX_GROUP * tq, FOX_HEAD_DIM), F32),
                        pltpu.VMEM((FOX_GROUP * tq, FOX_HEAD_DIM), F32),
                        pltpu.VMEM((FOX_GROUP * tq, FOX_HEAD_DIM), F32)],
    )
    return pl.pallas_call(
        functools.partial(_fox_kernel, tq=tq),
        grid_spec=grid_spec,
        out_shape=jax.ShapeDtypeStruct((b, l, FOX_WIDTH), BF16),
        compiler_params=_params("parallel", "parallel", "arbitrary"),
        name="fox_prompt",
    )(qi_tab, kj_tab, proj, proj, proj, cum4)


def _foxdec_kernel(pt_ref, q_ref, kn_ref, vn_ref, lfn_ref, *refs, n_pages):
    k_refs = refs[:n_pages]
    v_refs = refs[n_pages:2 * n_pages]
    lf_refs = refs[2 * n_pages:3 * n_pages]
    o_ref = refs[3 * n_pages]
    del pt_ref
    scale = FOX_HEAD_DIM ** -0.5
    cols = k_refs[0].shape[0]
    page = cols // FOX_KV_HEADS
    q = q_ref[...] * scale
    qb = q.astype(BF16)
    valid = (_iota((FOX_HEADS, cols), 1) % FOX_KV_HEADS) == (_iota((FOX_HEADS, cols), 0) // FOX_GROUP)
    spread = (_iota((page, cols), 0) <= _iota((page, cols), 1) // FOX_KV_HEADS).astype(F32)

    off = jnp.zeros((FOX_HEADS, 1), F32)
    logits = []
    for p in range(n_pages):
        lf = lf_refs[p][...]
        cum = jnp.dot(lf, spread, precision=HI, preferred_element_type=F32) + off
        off = off + jnp.sum(lf, axis=1, keepdims=True)
        sp = lax.dot_general(qb, k_refs[p][...].astype(BF16), NT_DIMS, preferred_element_type=F32)
        logits.append(jnp.where(valid, sp - cum, -jnp.inf))
    s_new = jnp.sum(q * kn_ref[...], axis=1, keepdims=True) - (off + lfn_ref[...])

    m = s_new
    for lg in logits:
        m = jnp.maximum(m, jnp.max(lg, axis=1, keepdims=True))
    p_new = jnp.exp(s_new - m)
    l = p_new
    acc = p_new * vn_ref[...]
    for p in range(n_pages):
        pp = jnp.exp(logits[p] - m)
        l = l + jnp.sum(pp, axis=1, keepdims=True)
        acc = acc + jnp.dot(pp.astype(BF16), v_refs[p][...].astype(BF16), preferred_element_type=F32)
    o_ref[...] = (acc / l).astype(o_ref.dtype)


def _fox_decode(q, k_new, v_new, lf_new, cache_k, cache_v, cache_lft, page_table):
    db, n_pages = page_table.shape

    def page_spec(p, shape):
        return pl.BlockSpec((None,) + shape, lambda i, pt: (pt[i, p], 0, 0))

    row = pl.BlockSpec((None, FOX_HEADS, FOX_HEAD_DIM), lambda i, pt: (i, 0, 0))
    in_specs = [row, row, row, pl.BlockSpec((None, FOX_HEADS, 1), lambda i, pt: (i, 0, 0))]
    in_specs += [page_spec(p, cache_k.shape[1:]) for p in range(n_pages)]
    in_specs += [page_spec(p, cache_v.shape[1:]) for p in range(n_pages)]
    in_specs += [page_spec(p, cache_lft.shape[1:]) for p in range(n_pages)]
    grid_spec = pltpu.PrefetchScalarGridSpec(
        num_scalar_prefetch=1, grid=(db,), in_specs=in_specs,
        out_specs=pl.BlockSpec((None, FOX_HEADS, FOX_HEAD_DIM), lambda i, pt: (i, 0, 0)))
    return pl.pallas_call(
        functools.partial(_foxdec_kernel, n_pages=n_pages),
        grid_spec=grid_spec,
        out_shape=jax.ShapeDtypeStruct((db, FOX_HEADS, FOX_HEAD_DIM), BF16),
        compiler_params=_params("parallel"),
        name="fox_decode",
    )(page_table, q, k_new, v_new, lf_new, *([cache_k] * n_pages), *([cache_v] * n_pages),
      *([cache_lft] * n_pages))


def _gla_gate(gg, wup, bg):
    return _log_sigmoid(jnp.dot(gg, wup, precision=HI, preferred_element_type=F32) + bg) / GLA_GATE_TEMP


def _gla_out(o, g, gr):
    return _rms(o, g) * _silu(gr)


def _gla_kernel(q_ref, k_ref, v_ref, r_ref, sm_ref, wup_ref, bg_ref, gn_ref, o_ref, s_ref):
    @pl.when(pl.program_id(2) == 0)
    def _():
        s_ref[...] = jnp.zeros_like(s_ref)

    c = q_ref.shape[0]
    q = q_ref[...] * (GLA_DK ** -0.5)
    k = k_ref[...]
    v = v_ref[...]
    gg = sm_ref[:, FOX_HEADS:FOX_HEADS + GLA_GATE_RANK]
    la = _gla_gate(gg, wup_ref[...], bg_ref[...])
    tri = (_iota((c, c), 1) <= _iota((c, c), 0)).astype(F32)
    b = jnp.dot(tri, la, precision=HI, preferred_element_type=F32)
    b_last = b[c - 1:c, :]
    s_old = s_ref[...]
    vb = v.astype(BF16)
    o = jnp.dot((q * jnp.exp(b)).astype(BF16), s_old.astype(BF16), preferred_element_type=F32)

    rows = _iota((c, GLA_DK), 0)
    lane = _iota((c, c), 1)
    a = jnp.zeros((c, c), F32)
    for j in range(c):
        w = jnp.exp(jnp.where(rows >= j, b - b[j:j + 1, :], -jnp.inf))
        col = jnp.sum(q * w * k[j:j + 1, :], axis=1, keepdims=True)
        a = jnp.where(lane == j, col, a)
    o = o + jnp.dot(a.astype(BF16), vb, preferred_element_type=F32)
    o_ref[...] = _gla_out(o, gn_ref[...], r_ref[...]).astype(o_ref.dtype)

    kd = (k * jnp.exp(b_last - b)).astype(BF16)
    ones = jnp.ones((c, LANES), F32)
    decay = jnp.exp(lax.dot_general(la, ones, TN_DIMS, precision=HI, preferred_element_type=F32))
    decay = jnp.concatenate([decay] * (GLA_DV // LANES), axis=1)
    s_ref[...] = decay * s_old + lax.dot_general(kd, vb, TN_DIMS, preferred_element_type=F32)


def _gla_prompt(proj, small, wup, bg, gn):
    b, l, _ = proj.shape
    c = GLA_CHUNK if l % GLA_CHUNK == 0 else l
    qc, kc = COL_GQ // GLA_DK, COL_GK // GLA_DK
    vc, rc = COL_GV // GLA_DV, COL_GR // GLA_DV
    return pl.pallas_call(
        _gla_kernel,
        grid=(b, GLA_HEADS, l // c),
        in_specs=[pl.BlockSpec((None, c, GLA_DK), lambda i, h, t: (i, t, qc + h)),
                  pl.BlockSpec((None, c, GLA_DK), lambda i, h, t: (i, t, kc + h)),
                  pl.BlockSpec((None, c, GLA_DV), lambda i, h, t: (i, t, vc + h)),
                  pl.BlockSpec((None, c, GLA_DV), lambda i, h, t: (i, t, rc + h)),
                  pl.BlockSpec((None, c, SMALL_COLS), lambda i, h, t: (i, t, 0)),
                  pl.BlockSpec((GLA_GATE_RANK, GLA_DK), lambda i, h, t: (0, h)),
                  pl.BlockSpec((1, GLA_DK), lambda i, h, t: (0, h)),
                  pl.BlockSpec((1, GLA_DV), lambda i, h, t: (0, 0))],
        out_specs=[pl.BlockSpec((None, c, GLA_DV), lambda i, h, t: (i, t, h)),
                   pl.BlockSpec((None, None, GLA_DK, GLA_DV), lambda i, h, t: (i, h, 0, 0))],
        out_shape=[jax.ShapeDtypeStruct((b, l, GLA_WIDTH), BF16),
                   jax.ShapeDtypeStruct((b, GLA_HEADS, GLA_DK, GLA_DV), F32)],
        compiler_params=_params("parallel", "parallel", "arbitrary"),
        name="gla_prompt",
    )(proj, proj, proj, proj, small, wup, bg, gn)


def _gladec_kernel(q_ref, k_ref, v_ref, r_ref, sm_ref, wup_ref, bg_ref, gn_ref, s_ref, o_ref, so_ref):
    gg = sm_ref[:, FOX_HEADS:FOX_HEADS + GLA_GATE_RANK]
    la = _gla_gate(gg, wup_ref[...], bg_ref[...])
    eye = (_iota((GLA_DK, GLA_DK), 0) == _iota((GLA_DK, GLA_DK), 1)).astype(F32)
    for h in range(GLA_HEADS):
        ks = slice(h * GLA_DK, (h + 1) * GLA_DK)
        vs = slice(h * GLA_DV, (h + 1) * GLA_DV)
        rows3 = jnp.concatenate([jnp.exp(la[:, ks]), k_ref[:, ks], q_ref[:, ks] * (GLA_DK ** -0.5),
                                 jnp.zeros((SUBLANES - 3, GLA_DK), F32)], axis=0)
        cols = lax.dot_general(eye, rows3, NT_DIMS, precision=HI, preferred_element_type=F32)
        s_new = cols[:, 0:1] * s_ref[h] + cols[:, 1:2] * v_ref[:, vs]
        so_ref[h] = s_new
        o = jnp.sum(cols[:, 2:3] * s_new, axis=0, keepdims=True)
        o_ref[:, vs] = _gla_out(o, gn_ref[...], r_ref[:, vs]).astype(o_ref.dtype)


def _gla_decode(gq, gk, gv, gr, small3, wup, bg, gn, state):
    db = gq.shape[0]
    st =pl.BlockSpec((None, GLA_HEADS, GLA_DK, GLA_DV), lambda i: (i, 0, 0, 0))
    return pl.pallas_call(
        _gladec_kernel,
        grid=(db,),
        in_specs=[pl.BlockSpec((None, 1, GLA_QK_WIDTH), lambda i: (i, 0, 0)),
                  pl.BlockSpec((None, 1, GLA_QK_WIDTH), lambda i: (i, 0, 0)),
                  pl.BlockSpec((None, 1, GLA_WIDTH), lambda i: (i, 0, 0)),
                  pl.BlockSpec((None, 1, GLA_WIDTH), lambda i: (i, 0, 0)),
                  pl.BlockSpec((None, 1, SMALL_COLS), lambda i: (i, 0, 0)),
                  pl.BlockSpec((GLA_GATE_RANK, GLA_QK_WIDTH), lambda i: (0, 0)),
                  pl.BlockSpec((1, GLA_QK_WIDTH), lambda i: (0, 0)),
                  pl.BlockSpec((1, GLA_DV), lambda i: (0, 0)),
                  st],
        out_specs=[pl.BlockSpec((None, 1, GLA_WIDTH), lambda i: (i, 0, 0)), st],
        out_shape=[jax.ShapeDtypeStruct((db, 1, GLA_WIDTH), BF16),
                   jax.ShapeDtypeStruct(state.shape, F32)],
        compiler_params=_params("parallel"),
        name="gla_decode",
    )(gq, gk, gv, gr, small3, wup, bg, gn, state)


def _outproj_kernel(a_ref, b_ref, wa_ref, wb_ref, x_ref, g_ref, gt_ref, o_ref, acc_ref):
    j = pl.program_id(1)
    nj = pl.num_programs(1)
    acc_ref[j] = (jnp.dot(a_ref[...], wa_ref[...], preferred_element_type=F32)
                  + jnp.dot(b_ref[...], wb_ref[...], preferred_element_type=F32))

    @pl.when(j == nj - 1)
    def _():
        n_chunks, _, tn = acc_ref.shape
        ss = jnp.zeros((acc_ref.shape[1], 1), F32)
        for c in range(n_chunks):
            z = acc_ref[c]
            ss = ss + jnp.sum(z * z, axis=1, keepdims=True)
        rs = lax.rsqrt(ss / (n_chunks * tn) + NORM_EPS)
        for c in range(n_chunks):
            cs = slice(c * tn, (c + 1) * tn)
            o_ref[:, cs] = x_ref[:, cs] + gt_ref[:, cs] * (acc_ref[c] * rs * g_ref[:, cs])


def _out_proj(o_fox, o_gla, w_out, x, g, mod, per_row, rows_per_seq, tm, tn=512):
    t, d = x.shape
    ka = o_fox.shape[1]
    tn = min(tn, d)
    tps = max(rows_per_seq // tm, 1)
    return pl.pallas_call(
        _outproj_kernel,
        grid=(t // tm, d // tn),
        in_specs=[pl.BlockSpec((tm, ka), lambda i, j: (i, 0)),
                  pl.BlockSpec((tm, ka), lambda i, j: (i, 0)),
                  pl.BlockSpec((ka, tn), lambda i, j: (0, j)),
                  pl.BlockSpec((ka, tn), lambda i, j: (1, j)),
                  pl.BlockSpec((tm, d), lambda i, j: (i, 0), pipeline_mode=pl.Buffered(1)),
                  pl.BlockSpec((1, d), lambda i, j: (0, 0)),
                  _mod_spec(per_row, 2, tm, d, tps)],
        out_specs=pl.BlockSpec((tm, d), lambda i, j: (i, 0)),
        out_shape=jax.ShapeDtypeStruct((t, d), F32),
        scratch_shapes=[pltpu.VMEM((d // tn, tm, tn), F32)],
        compiler_params=_params("parallel", "arbitrary"),
        name="out_proj",
    )(o_fox, o_gla, w_out, w_out, x, g, mod)


def _ffnpre_kernel(x_ref, g_ref, sh_ref, sc_ref, wr_ref, t_ref, lg_ref):
    h = _rms(x_ref[...], g_ref[...]) * (1.0 + sc_ref[...]) + sh_ref[...]
    t_ref[...] = h.astype(BF16)
    lg_ref[...] = jnp.dot(h, wr_ref[...], precision=HI, preferred_element_type=F32)


def _ffn_pre(x, g, mod, per_row, rows_per_seq, w_router, tm):
    t, d = x.shape
    tps = max(rows_per_seq // tm, 1)
    return pl.pallas_call(
        _ffnpre_kernel,
        grid=(t // tm,),
        in_specs=[pl.BlockSpec((tm, d), lambda i: (i, 0)),
                  pl.BlockSpec((1, d), lambda i: (0, 0)),
                  _mod_spec(per_row, 3, tm, d, tps),
                  _mod_spec(per_row, 4, tm, d, tps),
                  pl.BlockSpec((d, N_EXPERTS), lambda i: (0, 0))],
        out_specs=[pl.BlockSpec((tm, d), lambda i: (i, 0)),
                   pl.BlockSpec((tm, N_EXPERTS), lambda i: (i, 0))],
        out_shape=[jax.ShapeDtypeStruct((t, d), BF16),
                   jax.ShapeDtypeStruct((t, N_EXPERTS), F32)],
        compiler_params=_params("parallel"),
        name="ffn_pre",
    )(x, g, mod, mod, w_router)


def _new_expert(be_ref, i):
    return jnp.logical_or(i == 0, be_ref[i] != be_ref[jnp.maximum(i - 1, 0)])


def _stream_weights(be_ref, nu_ref, run_ref, rexp_ref, nr_ref, copies, cast):
    p = pl.program_id(0)
    i = pl.program_id(1)

    @pl.when(jnp.logical_and(i < nu_ref[0], _new_expert(be_ref, i)))
    def _():
        run = run_ref[i]
        last = run + 1 == nr_ref[0]

        @pl.when(jnp.logical_and(p == 0, i == 0))
        def _():
            for c in copies(p, be_ref[i]):
                c.start()

        for c in copies(p, be_ref[i]):
            c.wait()
        cast()

        @pl.when(jnp.logical_or(jnp.logical_not(last), p + 1 < pl.num_programs(0)))
        def _():
            for c in copies(jnp.where(last, p + 1, p), rexp_ref[jnp.where(last, 0, run + 1)]):
                c.start()


def _gateup_kernel(be_ref, nu_ref, run_ref, rexp_ref, nr_ref, x_ref, wg_hbm, wu_hbm, h_ref,
                   wgf_ref, wuf_ref, wgb_ref, wub_ref, sem):
    tf = wgf_ref.shape[1]

    def copies(p, e):
        cols = pl.ds(pl.multiple_of(p * tf, tf), tf)
        return (pltpu.make_async_copy(wg_hbm.at[0, e, :, cols], wgf_ref, sem.at[0]),
                pltpu.make_async_copy(wu_hbm.at[0, e, :, cols], wuf_ref, sem.at[1]))

    def cast():
        wgb_ref[...] = wgf_ref[...].astype(BF16)
        wub_ref[...] = wuf_ref[...].astype(BF16)

    _stream_weights(be_ref, nu_ref, run_ref, rexp_ref, nr_ref, copies, cast)

    @pl.when(pl.program_id(1) < nu_ref[0])
    def _():
        x = x_ref[...]
        g = jnp.dot(x, wgb_ref[...], preferred_element_type=F32)
        u = jnp.dot(x, wub_ref[...], preferred_element_type=F32)
        h_ref[...] = (_silu(g) * u).astype(h_ref.dtype)

    @pl.when(pl.program_id(1) >= nu_ref[0])
    def _():
        h_ref[...] = jnp.zeros_like(h_ref)


def _expert_gate_up(tables, x_sorted, w_gate, w_up, tm, tf):
    r, d = x_sorted.shape
    ff = w_gate.shape[-1]
    tf = min(tf, ff)
    grid_spec = pltpu.PrefetchScalarGridSpec(
        num_scalar_prefetch=len(tables), grid=(ff // tf, r // tm),
        in_specs=[pl.BlockSpec((tm, d), lambda p, i, *_: (i, 0)),
                  pl.BlockSpec(memory_space=pl.ANY), pl.BlockSpec(memory_space=pl.ANY)],
        out_specs=pl.BlockSpec((tm, tf), lambda p, i, *_: (i, p)),
        scratch_shapes=[pltpu.VMEM((d, tf), F32), pltpu.VMEM((d, tf), F32),
                        pltpu.VMEM((d, tf), BF16), pltpu.VMEM((d, tf), BF16),
                        pltpu.SemaphoreType.DMA((2,))])
    return pl.pallas_call(
        _gateup_kernel, grid_spec=grid_spec,
        out_shape=jax.ShapeDtypeStruct((r, ff), BF16),
        compiler_params=_params("arbitrary", "arbitrary"),
        name="expert_gate_up",
    )(*tables, x_sorted, w_gate, w_up)


def _down_kernel(be_ref, nu_ref, run_ref, rexp_ref, nr_ref, h_ref, sw_ref, wd_hbm, y_ref, wdf_ref, wdb_ref, sem):
    tn = wdf_ref.shape[1]

    def copies(p, e):
        cols = pl.ds(pl.multiple_of(p * tn, tn), tn)
        return (pltpu.make_async_copy(wd_hbm.at[0, e, :, cols], wdf_ref, sem.at[0]),)

    def cast():
        wdb_ref[...] = wdf_ref[...].astype(BF16)

    _stream_weights(be_ref, nu_ref, run_ref, rexp_ref, nr_ref, copies, cast)

    @pl.when(pl.program_id(1) < nu_ref[0])
    def _():
        y = jnp.dot(h_ref[...], wdb_ref[...], preferred_element_type=F32)
        y_ref[...] = (y * sw_ref[...]).astype(y_ref.dtype)

    @pl.when(pl.program_id(1) >= nu_ref[0])
    def _():
        y_ref[...] = jnp.zeros_like(y_ref)


def _expert_down(tables, h_sorted, slot_w, w_down, tm, tn):
    r, ff = h_sorted.shape
    d = w_down.shape[-1]
    tn = min(tn, d)
    grid_spec = pltpu.PrefetchScalarGridSpec(
        num_scalar_prefetch=len(tables), grid=(d // tn, r // tm),
        in_specs=[pl.BlockSpec((tm, ff), lambda p, i, *_: (i, 0)),
                  pl.BlockSpec((tm, 1), lambda p, i, *_: (i, 0)),
                  pl.BlockSpec(memory_space=pl.ANY)],
        out_specs=pl.BlockSpec((tm, tn), lambda p, i, *_: (i, p)),
        scratch_shapes=[pltpu.VMEM((ff, tn), F32), pltpu.VMEM((ff, tn), BF16), pltpu.SemaphoreType.DMA((1,))])
    return pl.pallas_call(
        _down_kernel, grid_spec=grid_spec,
        out_shape=jax.ShapeDtypeStruct((r, d), BF16),
        compiler_params=_params("arbitrary", "arbitrary"),
        name="expert_down",
    )(*tables, h_sorted, slot_w, w_down)


def _shared_kernel(t_ref, wg_ref, wu_ref, wd_ref, o_ref):
    f = pl.program_id(1)
    t = t_ref[...]
    h = _silu(jnp.dot(t, wg_ref[...], preferred_element_type=F32)) * jnp.dot(t, wu_ref[...], preferred_element_type=F32)
    y = jnp.dot(h.astype(BF16), wd_ref[...], preferred_element_type=F32)

    @pl.when(f == 0)
    def _():
        o_ref[...] = y

    @pl.when(f > 0)
    def _():
        o_ref[...] += y


def _shared_mlp(t, wg, wu, wd, tm, tf=256):
    n, d = t.shape
    ff = wg.shape[1]
    return pl.pallas_call(
        _shared_kernel,
        grid=(n // tm, ff // tf),
        in_specs=[pl.BlockSpec((tm, d), lambda i, f: (i, 0)),
                  pl.BlockSpec((d, tf), lambda i, f: (0, f)),
                  pl.BlockSpec((d, tf), lambda i, f: (0, f)),
                  pl.BlockSpec((tf, d), lambda i, f: (f, 0))],
        out_specs=pl.BlockSpec((tm, d), lambda i, f: (i, 0)),
        out_shape=jax.ShapeDtypeStruct((n, d), F32),
        compiler_params=_params("parallel", "arbitrary"),
        name="shared_mlp",
    )(t, wg, wu, wd)


def _combine_kernel(y_ref, sh_ref, x_ref, g_ref, gt_ref, o_ref):
    z = sh_ref[...]
    for k in range(y_ref.shape[0]):
        z = z + y_ref[k].astype(F32)
    o_ref[...] = x_ref[...] + gt_ref[...] * _rms(z, g_ref[...])


def _combine_final(gathered, row_block0, shared, x, g, mod, per_row, rows_per_seq, tm):
    n, d = x.shape
    tps = max(rows_per_seq // tm, 1)
    return pl.pallas_call(
        _combine_kernel,
        grid=(n // tm,),
        in_specs=[pl.BlockSpec((TOP_K, tm, d), lambda i: (0, row_block0 + i, 0)),
                  pl.BlockSpec((tm, d), lambda i: (i, 0)),
                  pl.BlockSpec((tm, d), lambda i: (i, 0)),
                  pl.BlockSpec((1, d), lambda i: (0, 0)),
                  _mod_spec(per_row, 5, tm, d, tps)],
        out_specs=pl.BlockSpec((tm, d), lambda i: (i, 0)),
        out_shape=jax.ShapeDtypeStruct((n, d), F32),
        compiler_params=_params("parallel"),
        name="combine_final",
    )(gathered, shared, x, g, mod)


def _first_max(x, lane, n):
    m = jnp.max(x, axis=1, keepdims=True)
    return m, jnp.min(jnp.where(x == m, lane, n), axis=1, keepdims=True)


def _route_kernel(lg_ref, br_ref, idx_ref, w_ref, rank_ref, cnt_ref, carry_ref):
    @pl.when(pl.program_id(0) == 0)
    def _():
        carry_ref[...] = jnp.zeros_like(carry_ref)

    tm = lg_ref.shape[0]
    scores = _sigmoid(lg_ref[...])
    sel = scores + br_ref[...]
    lane = _iota((tm, N_EXPERTS), 1).astype(F32)
    grp = (_iota((tm, N_EXPERTS), 1) // EXPERTS_PER_GROUP).astype(F32)
    neg = -jnp.inf

    gsc = jnp.full((tm, N_EXPERTS), neg, F32)
    for g in range(N_GROUPS):
        mg = jnp.where(grp == g, sel, neg)
        m1, i1 = _first_max(mg, lane, N_EXPERTS)
        m2 = jnp.max(jnp.where(lane == i1, neg, mg), axis=1, keepdims=True)
        gsc = jnp.where(lane == g, m1 + m2, gsc)
    cand = jnp.full((tm, N_EXPERTS), neg, F32)
    for _ in range(TOPK_GROUPS):
        _, gi = _first_max(gsc, lane, N_EXPERTS)
        cand = jnp.where(grp == gi, sel, cand)
        gsc = jnp.where(lane == gi, neg, gsc)

    col = _iota((tm, TOP_K), 1)
    idx = jnp.zeros((tm, TOP_K), F32)
    wts = jnp.zeros((tm, TOP_K), F32)
    chosen = jnp.zeros((tm, N_EXPERTS), F32)
    hits = []
    for k in range(TOP_K):
        _, ik = _first_max(cand, lane, N_EXPERTS)
        hit = lane == ik
        hits.append(hit)
        cand = jnp.where(hit, neg, cand)
        chosen = jnp.where(hit, 1.0, chosen)
        idx = jnp.where(col == k, ik, idx)
        wts = jnp.where(col == k, jnp.sum(jnp.where(hit, scores, 0.0), axis=1, keepdims=True), wts)
    idx_ref[...] = idx.astype(jnp.int32)
    w_ref[...] = wts / jnp.sum(wts, axis=1, keepdims=True) * ROUTED_SCALE

    below = (_iota((tm, tm), 1) < _iota((tm, tm), 0)).astype(BF16)
    prefix = jnp.dot(below, chosen.astype(BF16), preferred_element_type=F32) + carry_ref[...]
    rank = jnp.zeros((tm, TOP_K), F32)
    for k in range(TOP_K):
        rank = jnp.where(col == k, jnp.sum(jnp.where(hits[k], prefix, 0.0), axis=1, keepdims=True), rank)
    rank_ref[...] = rank.astype(jnp.int32)
    carry_ref[...] += jnp.sum(chosen, axis=0, keepdims=True)
    cnt_ref[...] = carry_ref[...]


def _route(logits, b_router):
    t = logits.shape[0]
    tm = max(m for m in range(SUBLANES, 1025, SUBLANES) if t % m == 0)
    kspec = pl.BlockSpec((tm, TOP_K), lambda i: (i, 0))
    return pl.pallas_call(
        _route_kernel,
        grid=(t // tm,),
        in_specs=[pl.BlockSpec((tm, N_EXPERTS), lambda i: (i, 0)),
                  pl.BlockSpec((1, N_EXPERTS), lambda i: (0, 0))],
        out_specs=[kspec, kspec, kspec, pl.BlockSpec((1, N_EXPERTS), lambda i: (0, 0))],
        out_shape=[jax.ShapeDtypeStruct((t, TOP_K), jnp.int32),
                   jax.ShapeDtypeStruct((t, TOP_K), F32),
                   jax.ShapeDtypeStruct((t, TOP_K), jnp.int32),
                   jax.ShapeDtypeStruct((1, N_EXPERTS), F32)],
        scratch_shapes=[pltpu.VMEM((1, N_EXPERTS), F32)],
        compiler_params=_params("arbitrary"),
        name="route",
    )(logits, b_router.reshape(1, N_EXPERTS))


def _dispatch(idx, wts, rank, counts, tm):
    t = idx.shape[0]
    m = t * TOP_K
    counts = counts.reshape(N_EXPERTS).astype(jnp.int32)
    padded = (counts + tm - 1) // tm * tm
    pad_end = jnp.cumsum(padded)
    pad_start = pad_end - padded
    dest = pad_start[idx] + rank
    n_blocks = -(-(m + N_EXPERTS * (tm - 1)) // tm)
    pairs = jnp.stack([jnp.repeat(jnp.arange(t, dtype=jnp.int32), TOP_K),
                       lax.bitcast_convert_type(wts.reshape(-1), jnp.int32)], axis=-1)
    n_slots = n_blocks * tm
    init = jnp.stack([jnp.arange(n_slots, dtype=jnp.int32) % t, jnp.zeros((n_slots,), jnp.int32)], axis=-1)
    slots = init.at[dest.reshape(-1)].set(pairs, mode="promise_in_bounds", unique_indices=True)
    slot_tok = slots[:, 0]
    slot_w = lax.bitcast_convert_type(slots[:, 1], F32).reshape(-1, 1)
    starts = jnp.arange(n_blocks, dtype=jnp.int32) * tm
    block_expert = jnp.minimum(jnp.sum((pad_end[None, :] <= starts[:, None]).astype(jnp.int32), axis=1), N_EXPERTS - 1)
    n_used = (pad_end[-1] // tm).astype(jnp.int32).reshape(1)
    present = counts > 0
    run_of_expert = jnp.cumsum(present.astype(jnp.int32)) - 1
    run_expert = jnp.argsort(jnp.logical_not(present), stable=True).astype(jnp.int32)
    n_runs = jnp.sum(present.astype(jnp.int32)).reshape(1)
    tables = (block_expert, n_used, run_of_expert[block_expert], run_expert, n_runs)
    return dest, slot_tok, slot_w, tables


def _tile(n, pref):
    return pref if n % pref == 0 else n


def kernel(x_prompt, x_sample, c_prompt, c_sample, cache_k, cache_v, cache_logf, state_gla, page_table, w_ada, b_ada, g_attn_pre, g_attn_post, g_ffn_pre, g_ffn_post, w_in, b_forget, w_gla_gate_up, b_gla_gate, g_gla_out, w_out, w_router, b_router, w_e_gate, w_e_up, w_e_down, w_s_gate, w_s_up, w_s_down):
    bsz, seq, d = x_prompt.shape
    db = x_sample.shape[0]
    depth = w_ada.shape[0]
    n_pool, page = cache_k.shape[1], cache_k.shape[2]
    tp, ts = bsz * seq, db
    y_p = x_prompt.reshape(tp, d)
    y_s = x_sample.reshape(ts, d)
    outs = [[] for _ in range(8)]

    n_mod = bsz + db
    mp = -(-n_mod // SUBLANES) * SUBLANES
    c_all = jnp.concatenate([c_prompt, c_sample, jnp.zeros((mp - n_mod, d), F32)], axis=0)

    o_fg = FOX_WIDTH + 2 * FOX_KV_WIDTH
    o_gq = o_fg + FOX_HEADS
    o_gg = o_gq + 2 * GLA_QK_WIDTH + GLA_WIDTH
    o_gr = o_gg + GLA_GATE_RANK

    for l in range(depth):
        mod = _modulation(c_all, w_ada[l], b_ada[l])
        mod_p = mod[:bsz].reshape(bsz, 6, 1, d)
        mod_s = mod[bsz:n_mod].reshape(db, 6, d).transpose(1, 0, 2)

        wl = w_in[l]
        w_big = jnp.concatenate([wl[:, :o_fg], wl[:, o_gq:o_gg], wl[:, o_gr:]], axis=1).astype(BF16)
        w_small = jnp.concatenate([wl[:, o_fg:o_gq], wl[:, o_gg:o_gr],
                                   jnp.zeros((d, SMALL_COLS - FOX_HEADS - GLA_GATE_RANK), F32)], axis=1).astype(BF16)
        b_small = jnp.concatenate([b_forget[l], jnp.zeros((SMALL_COLS - FOX_HEADS,), F32)]).reshape(1, SMALL_COLS)
        w_out_b = w_out[l].astype(BF16)
        wup = w_gla_gate_up[l]
        bg = b_gla_gate[l].reshape(1, GLA_QK_WIDTH)
        gn = g_gla_out[l].reshape(1, GLA_DV)
        g_ap = g_attn_pre[l].reshape(1, d)
        g_ao = g_attn_post[l].reshape(1, d)
        g_fp = g_ffn_pre[l].reshape(1, d)
        g_fo = g_ffn_post[l].reshape(1, d)

        tm_p = _tile(seq, 1024)
        proj, small = _in_proj(y_p, g_ap, mod_p, False, seq, w_big, w_small, b_small, tm_p)
        proj3 = proj.reshape(bsz, seq, BIG_COLS)
        small3 = small.reshape(bsz, seq, SMALL_COLS)
        cum = _cum_logf(small3, _tile(seq, 512))
        o_fox = _fox_prompt(proj3, cum.reshape(bsz, FOX_KV_HEADS, FOX_GROUP, seq), _tile(seq, 512))
        o_gla, s_fin = _gla_prompt(proj3, small3, wup, bg, gn)
        outs[0].append(proj3[:, :, COL_FK:COL_FV].reshape(bsz, seq, FOX_KV_HEADS, FOX_HEAD_DIM))
        outs[1].append(proj3[:, :, COL_FV:COL_GQ].reshape(bsz, seq, FOX_KV_HEADS, FOX_HEAD_DIM))
        outs[2].append(small3[:, :, :FOX_HEADS])
        outs[3].append(s_fin)
        y_p = _out_proj(o_fox.reshape(tp, FOX_WIDTH), o_gla.reshape(tp, GLA_WIDTH), w_out_b, y_p, g_ao,
                        mod_p, False, seq, _tile(seq, 512))

        proj_s, small_s = _in_proj(y_s, g_ap, mod_s, True, 1, w_big, w_small, b_small, ts)
        k_new = proj_s[:, COL_FK:COL_FV].reshape(db, FOX_KV_HEADS, FOX_HEAD_DIM)
        v_new = proj_s[:, COL_FV:COL_GQ].reshape(db, FOX_KV_HEADS, FOX_HEAD_DIM)
        lf_new = small_s[:, :FOX_HEADS]
        o_fox_s = _fox_decode(proj_s[:, :FOX_WIDTH].reshape(db, FOX_HEADS, FOX_HEAD_DIM),
                              jnp.repeat(k_new, FOX_GROUP, axis=1), jnp.repeat(v_new, FOX_GROUP, axis=1),
                              lf_new.reshape(db, FOX_HEADS, 1),
                              cache_k[l].reshape(n_pool, page * FOX_KV_HEADS, FOX_HEAD_DIM),
                              cache_v[l].reshape(n_pool, page * FOX_KV_HEADS, FOX_HEAD_DIM),
                              cache_logf[l].transpose(0, 2, 1), page_table)
        o_gla_s, s_new = _gla_decode(proj_s[:, COL_GQ:COL_GK].reshape(db, 1, GLA_QK_WIDTH),
                                     proj_s[:, COL_GK:COL_GV].reshape(db, 1, GLA_QK_WIDTH),
                                     proj_s[:, COL_GV:COL_GR].reshape(db, 1, GLA_WIDTH),
                                     proj_s[:, COL_GR:].reshape(db, 1, GLA_WIDTH),
                                     small_s.reshape(db, 1, SMALL_COLS), wup, bg, gn, state_gla[l])
        outs[4].append(k_new.reshape(db, 1, FOX_KV_HEADS, FOX_HEAD_DIM))
        outs[5].append(v_new.reshape(db, 1, FOX_KV_HEADS, FOX_HEAD_DIM))
        outs[6].append(lf_new.reshape(db, 1, FOX_HEADS))
        outs[7].append(s_new)
        y_s = _out_proj(o_fox_s.reshape(ts, FOX_WIDTH), o_gla_s.reshape(ts, GLA_WIDTH), w_out_b, y_s, g_ao,
                        mod_s, True, 1, ts)

        t_p, lg_p = _ffn_pre(y_p, g_fp, mod_p, False, seq, w_router[l], _tile(seq, 512))
        t_s, lg_s = _ffn_pre(y_s, g_fp, mod_s, True, 1, w_router[l], ts)
        t_all = jnp.concatenate([t_p, t_s], axis=0)
        idx, wts, rank, counts = _route(jnp.concatenate([lg_p, lg_s], axis=0), b_router[l])
        tm_e = 256
        dest, slot_tok, slot_w, tables = _dispatch(idx, wts, rank, counts, tm_e)
        x_sorted = t_all.at[slot_tok].get(mode="promise_in_bounds")
        h_sorted = _expert_gate_up(tables, x_sorted, w_e_gate[l:l + 1], w_e_up[l:l + 1], tm_e, tf=512)
        y_sorted = _expert_down(tables, h_sorted, slot_w, w_e_down[l:l + 1], tm_e, tn=4096)
        gathered = y_sorted.at[dest.T].get(mode="promise_in_bounds")
        wsg, wsu, wsd = w_s_gate[l].astype(BF16), w_s_up[l].astype(BF16), w_s_down[l].astype(BF16)
        sh_p = _shared_mlp(t_p, wsg, wsu, wsd, _tile(seq, 512))
        sh_s = _shared_mlp(t_s, wsg, wsu, wsd, ts)
        tm_c = math.gcd(LANES, ts)
        y_p = _combine_final(gathered, 0, sh_p, y_p, g_fo, mod_p, False, seq, tm_c)
        y_s = _combine_final(gathered, tp // tm_c, sh_s, y_s, g_fo, mod_s, True, 1, tm_c)

    k_p, v_p, f_p, s_p, k_s, v_s, f_s, s_s = [jnp.stack(o) for o in outs]
    return (y_p.reshape(bsz, seq, d), y_s.reshape(db, 1, d), k_p, v_p, f_p, s_p, k_s, v_s, f_s, s_s)
```

```python
import functools
import math

import numpy as np
import jax
import jax.numpy as jnp
from jax import lax
from jax.experimental import pallas as pl
from jax.experimental.pallas import tpu as pltpu

F32 = jnp.float32
BF16 = jnp.bfloat16
HI = lax.Precision.HIGHEST
NT_DIMS = (((1,), (1,)), ((), ()))
TN_DIMS = (((0,), (0,)), ((), ()))

FOX_HEADS = 16
FOX_KV_HEADS = 4
FOX_GROUP = FOX_HEADS // FOX_KV_HEADS
FOX_HEAD_DIM = 128
FOX_WIDTH = FOX_HEADS * FOX_HEAD_DIM
FOX_KV_WIDTH = FOX_KV_HEADS * FOX_HEAD_DIM
GLA_HEADS = 4
GLA_DK = 256
GLA_DV = 512
GLA_QK_WIDTH = GLA_HEADS * GLA_DK
GLA_WIDTH = GLA_HEADS * GLA_DV
GLA_GATE_RANK = 16
GLA_GATE_TEMP = 16.0
GLA_CHUNK = 64
GLA_SUB = 16
N_EXPERTS = 64
TOP_K = 8
N_GROUPS = 8
TOPK_GROUPS = 4
EXPERTS_PER_GROUP = N_EXPERTS // N_GROUPS
ROUTED_SCALE = 2.5
NORM_EPS = 1e-6
EXPERT_ROW_BLOCK = 256
EXPERT_ROW_SUB = 256

LANES = 128
SUBLANES = 8
VMEM_LIMIT_BYTES = 56 * 1024 * 1024

COL_FQ = 0
COL_FK = COL_FQ + FOX_WIDTH
COL_FV = COL_FK + FOX_KV_WIDTH
COL_GQ = COL_FV + FOX_KV_WIDTH
COL_GK = COL_GQ + GLA_QK_WIDTH
COL_GV = COL_GK + GLA_QK_WIDTH
COL_GR = COL_GV + GLA_WIDTH
BIG_COLS = COL_GR + GLA_WIDTH
SMALL_COLS = LANES


def _params(*sem):
    return pltpu.CompilerParams(dimension_semantics=sem, vmem_limit_bytes=VMEM_LIMIT_BYTES)


def _sigmoid(x):
    return 1.0 / (1.0 + jnp.exp(-x))


def _silu(x):
    return x * _sigmoid(x)


def _log_sigmoid(x):
    return jnp.minimum(x, 0.0) - jnp.log(1.0 + jnp.exp(-jnp.abs(x)))


def _rms(x, g):
    return x * lax.rsqrt(jnp.mean(x * x, axis=-1, keepdims=True) + NORM_EPS) * g


def _iota(shape, dim):
    return lax.broadcasted_iota(jnp.int32, shape, dim)


def _mod_kernel(c_ref, w_ref, b_ref, o_ref):
    a = _silu(c_ref[...]).astype(BF16)
    o_ref[...] = jnp.dot(a, w_ref[...].astype(BF16), preferred_element_type=F32) + b_ref[...]


def _modulation(c_all, w_ada, b_ada, tn=512):
    mp, d = c_all.shape
    n = w_ada.shape[1]
    return pl.pallas_call(
        _mod_kernel,
        grid=(n // tn,),
        in_specs=[pl.BlockSpec((mp, d), lambda j: (0, 0)),
                  pl.BlockSpec((d, tn), lambda j: (0, j)),
                  pl.BlockSpec((1, tn), lambda j: (0, j))],
        out_specs=pl.BlockSpec((mp, tn), lambda j: (0, j)),
        out_shape=jax.ShapeDtypeStruct((mp, n), F32),
        compiler_params=_params("parallel"),
        name="modulation",
    )(c_all, w_ada, b_ada.reshape(1, n))


def _mod_spec(per_row, which, tm, d, tiles_per_seq):
    if per_row:
        return pl.BlockSpec((None, tm, d), lambda i, *_: (which, i, 0))
    return pl.BlockSpec((None, None, 1, d), lambda i, *_: (i // tiles_per_seq, which, 0, 0))


def _inproj_kernel(x_ref, g_ref, sh_ref, sc_ref, w_ref, ws_ref, bs_ref, o_ref, os_ref, h_ref):
    @pl.when(pl.program_id(1) == 0)
    def _():
        h = _rms(x_ref[...], g_ref[...]) * (1.0 + sc_ref[...]) + sh_ref[...]
        hb = h.astype(BF16)
        h_ref[...] = hb
        sm = jnp.dot(hb, ws_ref[...], preferred_element_type=F32) + bs_ref[...]
        os_ref[...] = jnp.where(_iota(sm.shape, 1) < FOX_HEADS, _log_sigmoid(sm), sm)

    o_ref[...] = jnp.dot(h_ref[...], w_ref[...], preferred_element_type=F32)


def _in_proj(x, g, mod, per_row, rows_per_seq, w_big, w_small, b_small, tm, tn=512):
    t, d = x.shape
    tps = max(rows_per_seq // tm, 1)
    return pl.pallas_call(
        _inproj_kernel,
        grid=(t // tm, BIG_COLS // tn),
        in_specs=[pl.BlockSpec((tm, d), lambda i, j: (i, 0), pipeline_mode=pl.Buffered(1)),
                  pl.BlockSpec((1, d), lambda i, j: (0, 0)),
                  _mod_spec(per_row, 0, tm, d, tps),
                  _mod_spec(per_row, 1, tm, d, tps),
                  pl.BlockSpec((d, tn), lambda i, j: (0, j)),
                  pl.BlockSpec((d, SMALL_COLS), lambda i, j: (0, 0)),
                  pl.BlockSpec((1, SMALL_COLS), lambda i, j: (0, 0))],
        out_specs=[pl.BlockSpec((tm, tn), lambda i, j: (i, j)),
                   pl.BlockSpec((tm, SMALL_COLS), lambda i, j: (i, 0))],
        out_shape=[jax.ShapeDtypeStruct((t, BIG_COLS), F32),
                   jax.ShapeDtypeStruct((t, SMALL_COLS), F32)],
        scratch_shapes=[pltpu.VMEM((tm, d), BF16)],
        compiler_params=_params("parallel", "arbitrary"),
        name="in_proj",
    )(x, g, mod, mod, w_big, w_small, b_small)


def _cum_kernel(x_ref, o_ref, carry_ref):
    @pl.when(pl.program_id(1) == 0)
    def _():
        carry_ref[...] = jnp.zeros_like(carry_ref)

    x = x_ref[...]
    tl = x.shape[0]
    tri = (_iota((tl, tl), 1) <= _iota((tl, tl), 0)).astype(F32)
    cum = jnp.dot(tri, x, precision=HI, preferred_element_type=F32) + carry_ref[...]
    carry_ref[...] = cum[tl - 1:tl, :]
    o_ref[...] = cum.T[:FOX_HEADS, :]


def _cum_logf(small, tl):
    b, l, _ = small.shape
    return pl.pallas_call(
        _cum_kernel,
        grid=(b, l // tl),
        in_specs=[pl.BlockSpec((None, tl, SMALL_COLS), lambda i, t: (i, t, 0))],
        out_specs=pl.BlockSpec((None, FOX_HEADS, tl), lambda i, t: (i, 0, t)),
        out_shape=jax.ShapeDtypeStruct((b, FOX_HEADS, l), F32),
        scratch_shapes=[pltpu.VMEM((1, SMALL_COLS), F32)],
        compiler_params=_params("parallel", "arbitrary"),
        name="cum_logf",
    )(small)


def _fox_kernel(qi_tab, kj_tab, q_ref, k_ref, v_ref, ck_ref, o_ref, qs_ref, m_ref, l_ref, acc_ref, *, tq):
    p_id = pl.program_id(2)
    qi = qi_tab[p_id]
    kj = kj_tab[p_id]
    scale = FOX_HEAD_DIM ** -0.5

    @pl.when(kj == 0)
    def _():
        m_ref[...] = jnp.full_like(m_ref, -jnp.inf)
        l_ref[...] = jnp.zeros_like(l_ref)
        acc_ref[...] = jnp.zeros_like(acc_ref)
        for g in range(FOX_GROUP):
            qs_ref[g * tq:(g + 1) * tq, :] = (q_ref[:, g * FOX_HEAD_DIM:(g + 1) * FOX_HEAD_DIM] * scale).astype(BF16)

    def step(masked):
        kb = k_ref[...].astype(BF16)
        vb = v_ref[...].astype(BF16)
        s = lax.dot_general(qs_ref[...], kb, NT_DIMS, preferred_element_type=F32)
        ck = ck_ref[...]
        tk = kb.shape[0]
        parts = []
        for g in range(FOX_GROUP):
            sg = s[g * tq:(g + 1) * tq, :] - ck[g:g + 1, :]
            if masked:
                sg = jnp.where(_iota((tq, tk), 1) <= _iota((tq, tk), 0), sg, -jnp.inf)
            parts.append(sg)
        s = jnp.concatenate(parts, axis=0)
        m_prev = m_ref[...]
        m_next = jnp.maximum(m_prev, jnp.max(s, axis=1, keepdims=True))
        p = jnp.exp(s - m_next[:, :1])
        alpha = jnp.exp(m_prev - m_next)
        l_ref[...] = alpha * l_ref[...] + jnp.sum(p, axis=1, keepdims=True)
        m_ref[...] = m_next
        acc_ref[...] = alpha * acc_ref[...] + jnp.dot(p.astype(BF16), vb, preferred_element_type=F32)

    @pl.when(kj < qi)
    def _():
        step(False)

    @pl.when(kj == qi)
    def _():
        step(True)
        o = acc_ref[...] / l_ref[...]
        for g in range(FOX_GROUP):
            o_ref[:, g * FOX_HEAD_DIM:(g + 1) * FOX_HEAD_DIM] = o[g * tq:(g + 1) * tq, :].astype(o_ref.dtype)


def _fox_prompt(proj, cum4, tq):
    b, l, _ = proj.shape
    nq = l // tq
    pairs = [(i, j) for i in range(nq) for j in range(i + 1)]
    qi_tab = jnp.asarray(np.array([p[0] for p in pairs], np.int32))
    kj_tab = jnp.asarray(np.array([p[1] for p in pairs], np.int32))
    gw = FOX_GROUP * FOX_HEAD_DIM
    kcol = COL_FK // FOX_HEAD_DIM
    vcol = COL_FV // FOX_HEAD_DIM
    grid_spec = pltpu.PrefetchScalarGridSpec(
        num_scalar_prefetch=2,
        grid=(b, FOX_KV_HEADS, len(pairs)),
        in_specs=[pl.BlockSpec((None, tq, gw), lambda i, h, p, qt, kt: (i, qt[p], h)),
                  pl.BlockSpec((None, tq, FOX_HEAD_DIM), lambda i, h, p, qt, kt: (i, kt[p], kcol + h)),
                  pl.BlockSpec((None, tq, FOX_HEAD_DIM), lambda i, h, p, qt, kt: (i, kt[p], vcol + h)),
                  pl.BlockSpec((None, None, FOX_GROUP, tq), lambda i, h, p, qt, kt: (i, h, 0, kt[p]))],
        out_specs=pl.BlockSpec((None, tq, gw), lambda i, h, p, qt, kt: (i, qt[p], h)),
        scratch_shapes=[pltpu.VMEM((FOX_GROUP * tq, FOX_HEAD_DIM), BF16),
                        pltpu.VMEM((FOX_GROUP * tq, FOX_HEAD_DIM), F32),
                        pltpu.VMEM((FOX_GROUP * tq, FOX_HEAD_DIM), F32),
                        pltpu.VMEM((FOX_GROUP * tq, FOX_HEAD_DIM), F32)],
    )
    return pl.pallas_call(
        functools.partial(_fox_kernel, tq=tq),
        grid_spec=grid_spec,
        out_shape=jax.ShapeDtypeStruct((b, l, FOX_WIDTH), BF16),
        compiler_params=_params("parallel", "parallel", "arbitrary"),
        name="fox_prompt",
    )(qi_tab, kj_tab, proj, proj, proj, cum4)


def _foxdec_kernel(pt_ref, q_ref, kn_ref, vn_ref, lfn_ref, *refs, n_pages):
    k_refs = refs[:n_pages]
    v_refs = refs[n_pages:2 * n_pages]
    lf_refs = refs[2 * n_pages:3 * n_pages]
    o_ref = refs[3 * n_pages]
    del pt_ref
    scale = FOX_HEAD_DIM ** -0.5
    cols = k_refs[0].shape[0]
    page = cols // FOX_KV_HEADS
    q = q_ref[...] * scale
    qb = q.astype(BF16)
    valid = (_iota((FOX_HEADS, cols), 1) % FOX_KV_HEADS) == (_iota((FOX_HEADS, cols), 0) // FOX_GROUP)
    spread = (_iota((page, cols), 0) <= _iota((page, cols), 1) // FOX_KV_HEADS).astype(F32)

    off = jnp.zeros((FOX_HEADS, 1), F32)
    logits = []
    for p in range(n_pages):
        lf = lf_refs[p][...]
        cum = jnp.dot(lf, spread, precision=HI, preferred_element_type=F32) + off
        off = off + jnp.sum(lf, axis=1, keepdims=True)
        sp = lax.dot_general(qb, k_refs[p][...].astype(BF16), NT_DIMS, preferred_element_type=F32)
        logits.append(jnp.where(valid, sp - cum, -jnp.inf))
    s_new = jnp.sum(q * kn_ref[...], axis=1, keepdims=True) - (off + lfn_ref[...])

    m = s_new
    for lg in logits:
        m = jnp.maximum(m, jnp.max(lg, axis=1, keepdims=True))
    p_new = jnp.exp(s_new - m)
    l = p_new
    acc = p_new * vn_ref[...]
    for p in range(n_pages):
        pp = jnp.exp(logits[p] - m)
        l = l + jnp.sum(pp, axis=1, keepdims=True)
        acc = acc + jnp.dot(pp.astype(BF16), v_refs[p][...].astype(BF16), preferred_element_type=F32)
    o_ref[...] = (acc / l).astype(o_ref.dtype)


def _fox_decode(q, k_new, v_new, lf_new, cache_k, cache_v, cache_lft, page_table):
    db, n_pages = page_table.shape

    def page_spec(p, shape):
        return pl.BlockSpec((None,) + shape, lambda i, pt: (pt[i, p], 0, 0))

    row = pl.BlockSpec((None, FOX_HEADS, FOX_HEAD_DIM), lambda i, pt: (i, 0, 0))
    in_specs = [row, row, row, pl.BlockSpec((None, FOX_HEADS, 1), lambda i, pt: (i, 0, 0))]
    in_specs += [page_spec(p, cache_k.shape[1:]) for p in range(n_pages)]
    in_specs += [page_spec(p, cache_v.shape[1:]) for p in range(n_pages)]
    in_specs += [page_spec(p, cache_lft.shape[1:]) for p in range(n_pages)]
    grid_spec = pltpu.PrefetchScalarGridSpec(
        num_scalar_prefetch=1, grid=(db,), in_specs=in_specs,
        out_specs=pl.BlockSpec((None, FOX_HEADS, FOX_HEAD_DIM), lambda i, pt: (i, 0, 0)))
    return pl.pallas_call(
        functools.partial(_foxdec_kernel, n_pages=n_pages),
        grid_spec=grid_spec,
        out_shape=jax.ShapeDtypeStruct((db, FOX_HEADS, FOX_HEAD_DIM), BF16),
        compiler_params=_params("parallel"),
        name="fox_decode",
    )(page_table, q, k_new, v_new, lf_new, *([cache_k] * n_pages), *([cache_v] * n_pages),
      *([cache_lft] * n_pages))


def _gla_gate(gg, wup, bg):
    return _log_sigmoid(jnp.dot(gg, wup, precision=HI, preferred_element_type=F32) + bg) / GLA_GATE_TEMP


def _gla_out(o, g, gr):
    return _rms(o, g) * _silu(gr)


def _gla_kernel(q_ref, k_ref, v_ref, r_ref, sm_ref, wup_ref, bg_ref, gn_ref, o_ref, s_ref):
    @pl.when(pl.program_id(2) == 0)
    def _():
        s_ref[...] = jnp.zeros_like(s_ref)

    c = q_ref.shape[0]
    q = q_ref[...] * (GLA_DK ** -0.5)
    k = k_ref[...]
    v = v_ref[...]
    gg = sm_ref[:, FOX_HEADS:FOX_HEADS + GLA_GATE_RANK]
    la = _gla_gate(gg, wup_ref[...], bg_ref[...])
    tri = (_iota((c, c), 1) <= _iota((c, c), 0)).astype(F32)
    b = jnp.dot(tri, la, precision=HI, preferred_element_type=F32)
    b_last = b[c - 1:c, :]
    s_old = s_ref[...]
    vb = v.astype(BF16)
    o = jnp.dot((q * jnp.exp(b)).astype(BF16), s_old.astype(BF16), preferred_element_type=F32)

    sb = min(GLA_SUB, c)
    rows_c = _iota((c, GLA_DK), 0)
    rows_s = _iota((sb, GLA_DK), 0)
    lane = _iota((sb, c), 1)
    a_blocks = []
    for i0 in range(0, c, sb):
        qi = q[i0:i0 + sb, :]
        bi = b[i0:i0 + sb, :]
        a = jnp.zeros((sb, c), F32)
        if i0 > 0:
            r = b[i0:i0 + 1, :]
            qt = (qi * jnp.exp(bi - r)).astype(BF16)
            kt = (k * jnp.exp(jnp.where(rows_c < i0, r - b, -jnp.inf))).astype(BF16)
            a = lax.dot_general(qt, kt, NT_DIMS, preferred_element_type=F32)
        for jj in range(sb):
            j = i0 + jj
            w = jnp.exp(jnp.where(rows_s >= jj, bi - b[j:j + 1, :], -jnp.inf))
            col = jnp.sum(qi * w * k[j:j + 1, :], axis=1, keepdims=True)
            a = jnp.where(lane == j, col, a)
        a_blocks.append(a)
    a = jnp.concatenate(a_blocks, axis=0)
    o = o + jnp.dot(a.astype(BF16), vb, preferred_element_type=F32)
    o_ref[...] = _gla_out(o, gn_ref[...], r_ref[...]).astype(o_ref.dtype)

    kd = (k * jnp.exp(b_last - b)).astype(BF16)
    ones = jnp.ones((c, LANES), F32)
    decay = jnp.exp(lax.dot_general(la, ones, TN_DIMS, precision=HI, preferred_element_type=F32))
    decay = jnp.concatenate([decay] * (GLA_DV // LANES), axis=1)
    s_ref[...] = decay * s_old + lax.dot_general(kd, vb, TN_DIMS, preferred_element_type=F32)


def _gla_prompt(proj, small, wup, bg, gn):
    b, l, _ = proj.shape
    c = GLA_CHUNK if l % GLA_CHUNK == 0 else l
    qc, kc = COL_GQ // GLA_DK, COL_GK // GLA_DK
    vc, rc = COL_GV // GLA_DV, COL_GR // GLA_DV
    return pl.pallas_call(
        _gla_kernel,
        grid=(b, GLA_HEADS, l // c),
        in_specs=[pl.BlockSpec((None, c, GLA_DK), lambda i, h, t: (i, t, qc + h)),
                  pl.BlockSpec((None, c, GLA_DK), lambda i, h, t: (i, t, kc + h)),
                  pl.BlockSpec((None, c, GLA_DV), lambda i, h, t: (i, t, vc + h)),
                  pl.BlockSpec((None, c, GLA_DV), lambda i, h, t: (i, t, rc + h)),
                  pl.BlockSpec((None, c, SMALL_COLS), lambda i, h, t: (i, t, 0)),
                  pl.BlockSpec((GLA_GATE_RANK, GLA_DK), lambda i, h, t: (0, h)),
                  pl.BlockSpec((1, GLA_DK), lambda i, h, t: (0, h)),
                  pl.BlockSpec((1, GLA_DV), lambda i, h, t: (0, 0))],
        out_specs=[pl.BlockSpec((None, c, GLA_DV), lambda i, h, t: (i, t, h)),
                   pl.BlockSpec((None, None, GLA_DK, GLA_DV), lambda i, h, t: (i, h, 0, 0))],
        out_shape=[jax.ShapeDtypeStruct((b, l, GLA_WIDTH), BF16),
                   jax.ShapeDtypeStruct((b, GLA_HEADS, GLA_DK, GLA_DV), F32)],
        compiler_params=_params("parallel", "parallel", "arbitrary"),
        name="gla_prompt",
    )(proj, proj, proj, proj, small, wup, bg, gn)


def _gladec_kernel(q_ref, k_ref, v_ref, r_ref, sm_ref, wup_ref, bg_ref, gn_ref, s_ref, o_ref, so_ref):
    gg = sm_ref[:, FOX_HEADS:FOX_HEADS + GLA_GATE_RANK]
    la = _gla_gate(gg, wup_ref[...], bg_ref[...])
    eye = (_iota((GLA_DK, GLA_DK), 0) == _iota((GLA_DK, GLA_DK), 1)).astype(F32)
    for h in range(GLA_HEADS):
        ks = slice(h * GLA_DK, (h + 1) * GLA_DK)
        vs = slice(h * GLA_DV, (h + 1) * GLA_DV)
        rows3 = jnp.concatenate([jnp.exp(la[:, ks]), k_ref[:, ks], q_ref[:, ks] * (GLA_DK ** -0.5),
                                 jnp.zeros((SUBLANES - 3, GLA_DK), F32)], axis=0)
        cols = lax.dot_general(eye, rows3, NT_DIMS, precision=HI, preferred_element_type=F32)
        s_new = cols[:, 0:1] * s_ref[h] + cols[:, 1:2] * v_ref[:, vs]
        so_ref[h] = s_new
        o = jnp.sum(cols[:, 2:3] * s_new, axis=0, keepdims=True)
        o_ref[:, vs] = _gla_out(o, gn_ref[...], r_ref[:, vs]).astype(o_ref.dtype)


def _gla_decode(gq, gk, gv, gr, small3, wup, bg, gn, state):
    db = gq.shape[0]
    st =pl.BlockSpec((None, GLA_HEADS, GLA_DK, GLA_DV), lambda i: (i, 0, 0, 0))
    return pl.pallas_call(
        _gladec_kernel,
        grid=(db,),
        in_specs=[pl.BlockSpec((None, 1, GLA_QK_WIDTH), lambda i: (i, 0, 0)),
                  pl.BlockSpec((None, 1, GLA_QK_WIDTH), lambda i: (i, 0, 0)),
                  pl.BlockSpec((None, 1, GLA_WIDTH), lambda i: (i, 0, 0)),
                  pl.BlockSpec((None, 1, GLA_WIDTH), lambda i: (i, 0, 0)),
                  pl.BlockSpec((None, 1, SMALL_COLS), lambda i: (i, 0, 0)),
                  pl.BlockSpec((GLA_GATE_RANK, GLA_QK_WIDTH), lambda i: (0, 0)),
                  pl.BlockSpec((1, GLA_QK_WIDTH), lambda i: (0, 0)),
                  pl.BlockSpec((1, GLA_DV), lambda i: (0, 0)),
                  st],
        out_specs=[pl.BlockSpec((None, 1, GLA_WIDTH), lambda i: (i, 0, 0)), st],
        out_shape=[jax.ShapeDtypeStruct((db, 1, GLA_WIDTH), BF16),
                   jax.ShapeDtypeStruct(state.shape, F32)],
        compiler_params=_params("parallel"),
        name="gla_decode",
    )(gq, gk, gv, gr, small3, wup, bg, gn, state)


def _outproj_kernel(a_ref, b_ref, wa_ref, wb_ref, x_ref, g_ref, gt_ref, o_ref, acc_ref):
    j = pl.program_id(1)
    nj = pl.num_programs(1)
    acc_ref[j] = (jnp.dot(a_ref[...], wa_ref[...], preferred_element_type=F32)
                  + jnp.dot(b_ref[...], wb_ref[...], preferred_element_type=F32))

    @pl.when(j == nj - 1)
    def _():
        n_chunks, _, tn = acc_ref.shape
        ss = jnp.zeros((acc_ref.shape[1], 1), F32)
        for c in range(n_chunks):
            z = acc_ref[c]
            ss = ss + jnp.sum(z * z, axis=1, keepdims=True)
        rs = lax.rsqrt(ss / (n_chunks * tn) + NORM_EPS)
        for c in range(n_chunks):
            cs = slice(c * tn, (c + 1) * tn)
            o_ref[:, cs] = x_ref[:, cs] + gt_ref[:, cs] * (acc_ref[c] * rs * g_ref[:, cs])


def _out_proj(o_fox, o_gla, w_out, x, g, mod, per_row, rows_per_seq, tm, tn=512):
    t, d = x.shape
    ka = o_fox.shape[1]
    tn = min(tn, d)
    tps = max(rows_per_seq // tm, 1)
    return pl.pallas_call(
        _outproj_kernel,
        grid=(t // tm, d // tn),
        in_specs=[pl.BlockSpec((tm, ka), lambda i, j: (i, 0)),
                  pl.BlockSpec((tm, ka), lambda i, j: (i, 0)),
                  pl.BlockSpec((ka, tn), lambda i, j: (0, j)),
                  pl.BlockSpec((ka, tn), lambda i, j: (1, j)),
                  pl.BlockSpec((tm, d), lambda i, j: (i, 0), pipeline_mode=pl.Buffered(1)),
                  pl.BlockSpec((1, d), lambda i, j: (0, 0)),
                  _mod_spec(per_row, 2, tm, d, tps)],
        out_specs=pl.BlockSpec((tm, d), lambda i, j: (i, 0)),
        out_shape=jax.ShapeDtypeStruct((t, d), F32),
        scratch_shapes=[pltpu.VMEM((d // tn, tm, tn), F32)],
        compiler_params=_params("parallel", "arbitrary"),
        name="out_proj",
    )(o_fox, o_gla, w_out, w_out, x, g, mod)


def _ffnpre_kernel(x_ref, g_ref, sh_ref, sc_ref, wr_ref, t_ref, lg_ref):
    h = _rms(x_ref[...], g_ref[...]) * (1.0 + sc_ref[...]) + sh_ref[...]
    t_ref[...] = h.astype(BF16)
    lg_ref[...] = jnp.dot(h, wr_ref[...], precision=HI, preferred_element_type=F32)


def _ffn_pre(x, g, mod, per_row, rows_per_seq, w_router, tm):
    t, d = x.shape
    tps = max(rows_per_seq // tm, 1)
    return pl.pallas_call(
        _ffnpre_kernel,
        grid=(t // tm,),
        in_specs=[pl.BlockSpec((tm, d), lambda i: (i, 0)),
                  pl.BlockSpec((1, d), lambda i: (0, 0)),
                  _mod_spec(per_row, 3, tm, d, tps),
                  _mod_spec(per_row, 4, tm, d, tps),
                  pl.BlockSpec((d, N_EXPERTS), lambda i: (0, 0))],
        out_specs=[pl.BlockSpec((tm, d), lambda i: (i, 0)),
                   pl.BlockSpec((tm, N_EXPERTS), lambda i: (i, 0))],
        out_shape=[jax.ShapeDtypeStruct((t, d), BF16),
                   jax.ShapeDtypeStruct((t, N_EXPERTS), F32)],
        compiler_params=_params("parallel"),
        name="ffn_pre",
    )(x, g, mod, mod, w_router)


def _new_expert(be_ref, i):
    return jnp.logical_or(i == 0, be_ref[i] != be_ref[jnp.maximum(i - 1, 0)])


def _stream_weights(be_ref, nu_ref, run_ref, rexp_ref, nr_ref, copies, cast):
    p = pl.program_id(0)
    i = pl.program_id(1)

    @pl.when(jnp.logical_and(i < nu_ref[0], _new_expert(be_ref, i)))
    def _():
        run = run_ref[i]
        last = run + 1 == nr_ref[0]

        @pl.when(jnp.logical_and(p == 0, i == 0))
        def _():
            for c in copies(p, be_ref[i]):
                c.start()

        for c in copies(p, be_ref[i]):
            c.wait()
        cast()

        @pl.when(jnp.logical_or(jnp.logical_not(last), p + 1 < pl.num_programs(0)))
        def _():
            for c in copies(jnp.where(last, p + 1, p), rexp_ref[jnp.where(last, 0, run + 1)]):
                c.start()


def _by_valid_rows(nv, out_ref, compute):
    tm = out_ref.shape[0]
    sub = min(EXPERT_ROW_SUB, tm)
    for n in range(sub, tm + 1, sub):
        @pl.when(jnp.logical_and(nv > n - sub, nv <= n))
        def _(n=n):
            out_ref[:n, :] = compute(n)
            if n < tm:
                out_ref[n:, :] = jnp.zeros((tm - n, out_ref.shape[1]), out_ref.dtype)

    @pl.when(nv <= 0)
    def _():
        out_ref[...] = jnp.zeros_like(out_ref)


def _gateup_kernel(be_ref, nu_ref, run_ref, rexp_ref, nr_ref, nv_ref, x_ref, wg_hbm, wu_hbm, h_ref,
                   wgf_ref, wuf_ref, wgb_ref, wub_ref, sem):
    tf = wgf_ref.shape[1]

    def copies(p, e):
        cols = pl.ds(pl.multiple_of(p * tf, tf), tf)
        return (pltpu.make_async_copy(wg_hbm.at[0, e, :, cols], wgf_ref, sem.at[0]),
                pltpu.make_async_copy(wu_hbm.at[0, e, :, cols], wuf_ref, sem.at[1]))

    def cast():
        wgb_ref[...] = wgf_ref[...].astype(BF16)
        wub_ref[...] = wuf_ref[...].astype(BF16)

    _stream_weights(be_ref, nu_ref, run_ref, rexp_ref, nr_ref, copies, cast)

    def compute(n):
        x = x_ref[:n, :]
        g = jnp.dot(x, wgb_ref[...], preferred_element_type=F32)
        u = jnp.dot(x, wub_ref[...], preferred_element_type=F32)
        return (_silu(g) * u).astype(h_ref.dtype)

    _by_valid_rows(nv_ref[pl.program_id(1)], h_ref, compute)


def _expert_gate_up(tables, x_sorted, w_gate, w_up, tm, tf):
    r, d = x_sorted.shape
    ff = w_gate.shape[-1]
    tf = min(tf, ff)
    grid_spec = pltpu.PrefetchScalarGridSpec(
        num_scalar_prefetch=len(tables), grid=(ff // tf, r // tm),
        in_specs=[pl.BlockSpec((tm, d), lambda p, i, *_: (i, 0)),
                  pl.BlockSpec(memory_space=pl.ANY), pl.BlockSpec(memory_space=pl.ANY)],
        out_specs=pl.BlockSpec((tm, tf), lambda p, i, *_: (i, p)),
        scratch_shapes=[pltpu.VMEM((d, tf), F32), pltpu.VMEM((d, tf), F32),
                        pltpu.VMEM((d, tf), BF16), pltpu.VMEM((d, tf), BF16),
                        pltpu.SemaphoreType.DMA((2,))])
    return pl.pallas_call(
        _gateup_kernel, grid_spec=grid_spec,
        out_shape=jax.ShapeDtypeStruct((r, ff), BF16),
        compiler_params=_params("arbitrary", "arbitrary"),
        name="expert_gate_up",
    )(*tables, x_sorted, w_gate, w_up)


def _down_kernel(be_ref, nu_ref, run_ref, rexp_ref, nr_ref, nv_ref, h_ref, sw_ref, wd_hbm, y_ref, wdf_ref, wdb_ref,
                 sem):
    tn = wdf_ref.shape[1]

    def copies(p, e):
        cols = pl.ds(pl.multiple_of(p * tn, tn), tn)
        return (pltpu.make_async_copy(wd_hbm.at[0, e, :, cols], wdf_ref, sem.at[0]),)

    def cast():
        wdb_ref[...] = wdf_ref[...].astype(BF16)

    _stream_weights(be_ref, nu_ref, run_ref, rexp_ref, nr_ref, copies, cast)

    def compute(n):
        y = jnp.dot(h_ref[:n, :], wdb_ref[...], preferred_element_type=F32)
        return (y * sw_ref[:n, :]).astype(y_ref.dtype)

    _by_valid_rows(nv_ref[pl.program_id(1)], y_ref, compute)


def _expert_down(tables, h_sorted, slot_w, w_down, tm, tn):
    r, ff = h_sorted.shape
    d = w_down.shape[-1]
    tn = min(tn, d)
    grid_spec = pltpu.PrefetchScalarGridSpec(
        num_scalar_prefetch=len(tables), grid=(d // tn, r // tm),
        in_specs=[pl.BlockSpec((tm, ff), lambda p, i, *_: (i, 0)),
                  pl.BlockSpec((tm, 1), lambda p, i, *_: (i, 0)),
                  pl.BlockSpec(memory_space=pl.ANY)],
        out_specs=pl.BlockSpec((tm, tn), lambda p, i, *_: (i, p)),
        scratch_shapes=[pltpu.VMEM((ff, tn), F32), pltpu.VMEM((ff, tn), BF16), pltpu.SemaphoreType.DMA((1,))])
    return pl.pallas_call(
        _down_kernel, grid_spec=grid_spec,
        out_shape=jax.ShapeDtypeStruct((r, d), BF16),
        compiler_params=_params("arbitrary", "arbitrary"),
        name="expert_down",
    )(*tables, h_sorted, slot_w, w_down)


def _shared_kernel(t_ref, wg_ref, wu_ref, wd_ref, o_ref):
    f = pl.program_id(1)
    t = t_ref[...]
    h = _silu(jnp.dot(t, wg_ref[...], preferred_element_type=F32)) * jnp.dot(t, wu_ref[...], preferred_element_type=F32)
    y = jnp.dot(h.astype(BF16), wd_ref[...], preferred_element_type=F32)

    @pl.when(f == 0)
    def _():
        o_ref[...] = y

    @pl.when(f > 0)
    def _():
        o_ref[...] += y


def _shared_mlp(t, wg, wu, wd, tm, tf=256):
    n, d = t.shape
    ff = wg.shape[1]
    return pl.pallas_call(
        _shared_kernel,
        grid=(n // tm, ff // tf),
        in_specs=[pl.BlockSpec((tm, d), lambda i, f: (i, 0)),
                  pl.BlockSpec((d, tf), lambda i, f: (0, f)),
                  pl.BlockSpec((d, tf), lambda i, f: (0, f)),
                  pl.BlockSpec((tf, d), lambda i, f: (f, 0))],
        out_specs=pl.BlockSpec((tm, d), lambda i, f: (i, 0)),
        out_shape=jax.ShapeDtypeStruct((n, d), F32),
        compiler_params=_params("parallel", "arbitrary"),
        name="shared_mlp",
    )(t, wg, wu, wd)


def _combine_kernel(y_ref, sh_ref, x_ref, g_ref, gt_ref, o_ref):
    z = sh_ref[...]
    for k in range(y_ref.shape[0]):
        z = z + y_ref[k].astype(F32)
    o_ref[...] = x_ref[...] + gt_ref[...] * _rms(z, g_ref[...])


def _combine_final(gathered, row_block0, shared, x, g, mod, per_row, rows_per_seq, tm):
    n, d = x.shape
    tps = max(rows_per_seq // tm, 1)
    return pl.pallas_call(
        _combine_kernel,
        grid=(n // tm,),
        in_specs=[pl.BlockSpec((TOP_K, tm, d), lambda i: (0, row_block0 + i, 0)),
                  pl.BlockSpec((tm, d), lambda i: (i, 0)),
                  pl.BlockSpec((tm, d), lambda i: (i, 0)),
                  pl.BlockSpec((1, d), lambda i: (0, 0)),
                  _mod_spec(per_row, 5, tm, d, tps)],
        out_specs=pl.BlockSpec((tm, d), lambda i: (i, 0)),
        out_shape=jax.ShapeDtypeStruct((n, d), F32),
        compiler_params=_params("parallel"),
        name="combine_final",
    )(gathered, shared, x, g, mod)


def _first_max(x, lane, n):
    m = jnp.max(x, axis=1, keepdims=True)
    return m, jnp.min(jnp.where(x == m, lane, n), axis=1, keepdims=True)


def _route_kernel(lg_ref, br_ref, idx_ref, w_ref, rank_ref, cnt_ref, carry_ref):
    @pl.when(pl.program_id(0) == 0)
    def _():
        carry_ref[...] = jnp.zeros_like(carry_ref)

    tm = lg_ref.shape[0]
    scores = _sigmoid(lg_ref[...])
    sel = scores + br_ref[...]
    lane = _iota((tm, N_EXPERTS), 1).astype(F32)
    grp = (_iota((tm, N_EXPERTS), 1) // EXPERTS_PER_GROUP).astype(F32)
    neg = -jnp.inf

    gsc = jnp.full((tm, N_EXPERTS), neg, F32)
    for g in range(N_GROUPS):
        mg = jnp.where(grp == g, sel, neg)
        m1, i1 = _first_max(mg, lane, N_EXPERTS)
        m2 = jnp.max(jnp.where(lane == i1, neg, mg), axis=1, keepdims=True)
        gsc = jnp.where(lane == g, m1 + m2, gsc)
    cand = jnp.full((tm, N_EXPERTS), neg, F32)
    for _ in range(TOPK_GROUPS):
        _, gi = _first_max(gsc, lane, N_EXPERTS)
        cand = jnp.where(grp == gi, sel, cand)
        gsc = jnp.where(lane == gi, neg, gsc)

    col = _iota((tm, TOP_K), 1)
    idx = jnp.zeros((tm, TOP_K), F32)
    wts = jnp.zeros((tm, TOP_K), F32)
    chosen = jnp.zeros((tm, N_EXPERTS), F32)
    hits = []
    for k in range(TOP_K):
        _, ik = _first_max(cand, lane, N_EXPERTS)
        hit = lane == ik
        hits.append(hit)
        cand = jnp.where(hit, neg, cand)
        chosen = jnp.where(hit, 1.0, chosen)
        idx = jnp.where(col == k, ik, idx)
        wts = jnp.where(col == k, jnp.sum(jnp.where(hit, scores, 0.0), axis=1, keepdims=True), wts)
    idx_ref[...] = idx.astype(jnp.int32)
    w_ref[...] = wts / jnp.sum(wts, axis=1, keepdims=True) * ROUTED_SCALE

    below = (_iota((tm, tm), 1) < _iota((tm, tm), 0)).astype(BF16)
    prefix = jnp.dot(below, chosen.astype(BF16), preferred_element_type=F32) + carry_ref[...]
    rank = jnp.zeros((tm, TOP_K), F32)
    for k in range(TOP_K):
        rank = jnp.where(col == k, jnp.sum(jnp.where(hits[k], prefix, 0.0), axis=1, keepdims=True), rank)
    rank_ref[...] = rank.astype(jnp.int32)
    carry_ref[...] += jnp.sum(chosen, axis=0, keepdims=True)
    cnt_ref[...] = carry_ref[...]


def _route(logits, b_router):
    t = logits.shape[0]
    tm = max(m for m in range(SUBLANES, 1025, SUBLANES) if t % m == 0)
    kspec = pl.BlockSpec((tm, TOP_K), lambda i: (i, 0))
    return pl.pallas_call(
        _route_kernel,
        grid=(t // tm,),
        in_specs=[pl.BlockSpec((tm, N_EXPERTS), lambda i: (i, 0)),
                  pl.BlockSpec((1, N_EXPERTS), lambda i: (0, 0))],
        out_specs=[kspec, kspec, kspec, pl.BlockSpec((1, N_EXPERTS), lambda i: (0, 0))],
        out_shape=[jax.ShapeDtypeStruct((t, TOP_K), jnp.int32),
                   jax.ShapeDtypeStruct((t, TOP_K), F32),
                   jax.ShapeDtypeStruct((t, TOP_K), jnp.int32),
                   jax.ShapeDtypeStruct((1, N_EXPERTS), F32)],
        scratch_shapes=[pltpu.VMEM((1, N_EXPERTS), F32)],
        compiler_params=_params("arbitrary"),
        name="route",
    )(logits, b_router.reshape(1, N_EXPERTS))


def _dispatch(idx, wts, rank, counts, tm):
    t = idx.shape[0]
    m = t * TOP_K
    counts = counts.reshape(N_EXPERTS).astype(jnp.int32)
    padded = (counts + tm - 1) // tm * tm
    pad_end = jnp.cumsum(padded)
    pad_start = pad_end - padded
    dest = pad_start[idx] + rank
    n_blocks = -(-(m + N_EXPERTS * (tm - 1)) // tm)
    pairs = jnp.stack([jnp.repeat(jnp.arange(t, dtype=jnp.int32), TOP_K),
                       lax.bitcast_convert_type(wts.reshape(-1), jnp.int32)], axis=-1)
    n_slots = n_blocks * tm
    init = jnp.stack([jnp.arange(n_slots, dtype=jnp.int32) % t, jnp.zeros((n_slots,), jnp.int32)], axis=-1)
    slots = init.at[dest.reshape(-1)].set(pairs, mode="promise_in_bounds", unique_indices=True)
    slot_tok = slots[:, 0]
    slot_w = lax.bitcast_convert_type(slots[:, 1], F32).reshape(-1, 1)
    starts = jnp.arange(n_blocks, dtype=jnp.int32) * tm
    block_expert = jnp.minimum(jnp.sum((pad_end[None, :] <= starts[:, None]).astype(jnp.int32), axis=1), N_EXPERTS - 1)
    n_used = (pad_end[-1] // tm).astype(jnp.int32).reshape(1)
    present = counts > 0
    run_of_expert = jnp.cumsum(present.astype(jnp.int32)) - 1
    run_expert = jnp.argsort(jnp.logical_not(present), stable=True).astype(jnp.int32)
    n_runs = jnp.sum(present.astype(jnp.int32)).reshape(1)
    n_valid = jnp.clip(counts[block_expert] - (starts - pad_start[block_expert]), 0, tm)
    n_valid = jnp.where(starts < pad_end[-1], n_valid, 0).astype(jnp.int32)
    tables = (block_expert, n_used, run_of_expert[block_expert], run_expert, n_runs, n_valid)
    return dest, slot_tok, slot_w, tables


def _tile(n, pref):
    return pref if n % pref == 0 else n


def kernel(x_prompt, x_sample, c_prompt, c_sample, cache_k, cache_v, cache_logf, state_gla, page_table, w_ada, b_ada, g_attn_pre, g_attn_post, g_ffn_pre, g_ffn_post, w_in, b_forget, w_gla_gate_up, b_gla_gate, g_gla_out, w_out, w_router, b_router, w_e_gate, w_e_up, w_e_down, w_s_gate, w_s_up, w_s_down):
    bsz, seq, d = x_prompt.shape
    db = x_sample.shape[0]
    depth = w_ada.shape[0]
    n_pool, page = cache_k.shape[1], cache_k.shape[2]
    tp, ts = bsz * seq, db
    y_p = x_prompt.reshape(tp, d)
    y_s = x_sample.reshape(ts, d)
    outs = [[] for _ in range(8)]

    n_mod = bsz + db
    mp = -(-n_mod // SUBLANES) * SUBLANES
    c_all = jnp.concatenate([c_prompt, c_sample, jnp.zeros((mp - n_mod, d), F32)], axis=0)

    o_fg = FOX_WIDTH + 2 * FOX_KV_WIDTH
    o_gq = o_fg + FOX_HEADS
    o_gg = o_gq + 2 * GLA_QK_WIDTH + GLA_WIDTH
    o_gr = o_gg + GLA_GATE_RANK

    for l in range(depth):
        mod = _modulation(c_all, w_ada[l], b_ada[l])
        mod_p = mod[:bsz].reshape(bsz, 6, 1, d)
        mod_s = mod[bsz:n_mod].reshape(db, 6, d).transpose(1, 0, 2)

        wl = w_in[l]
        w_big = jnp.concatenate([wl[:, :o_fg], wl[:, o_gq:o_gg], wl[:, o_gr:]], axis=1).astype(BF16)
        w_small = jnp.concatenate([wl[:, o_fg:o_gq], wl[:, o_gg:o_gr],
                                   jnp.zeros((d, SMALL_COLS - FOX_HEADS - GLA_GATE_RANK), F32)], axis=1).astype(BF16)
        b_small = jnp.concatenate([b_forget[l], jnp.zeros((SMALL_COLS - FOX_HEADS,), F32)]).reshape(1, SMALL_COLS)
        w_out_b = w_out[l].astype(BF16)
        wup = w_gla_gate_up[l]
        bg = b_gla_gate[l].reshape(1, GLA_QK_WIDTH)
        gn = g_gla_out[l].reshape(1, GLA_DV)
        g_ap = g_attn_pre[l].reshape(1, d)
        g_ao = g_attn_post[l].reshape(1, d)
        g_fp = g_ffn_pre[l].reshape(1, d)
        g_fo = g_ffn_post[l].reshape(1, d)

        tm_p = _tile(seq, 1024)
        proj, small = _in_proj(y_p, g_ap, mod_p, False, seq, w_big, w_small, b_small, tm_p)
        proj3 = proj.reshape(bsz, seq, BIG_COLS)
        small3 = small.reshape(bsz, seq, SMALL_COLS)
        cum = _cum_logf(small3, _tile(seq, 512))
        o_fox = _fox_prompt(proj3, cum.reshape(bsz, FOX_KV_HEADS, FOX_GROUP, seq), _tile(seq, 1024))
        o_gla, s_fin = _gla_prompt(proj3, small3, wup, bg, gn)
        outs[0].append(proj3[:, :, COL_FK:COL_FV].reshape(bsz, seq, FOX_KV_HEADS, FOX_HEAD_DIM))
        outs[1].append(proj3[:, :, COL_FV:COL_GQ].reshape(bsz, seq, FOX_KV_HEADS, FOX_HEAD_DIM))
        outs[2].append(small3[:, :, :FOX_HEADS])
        outs[3].append(s_fin)
        y_p = _out_proj(o_fox.reshape(tp, FOX_WIDTH), o_gla.reshape(tp, GLA_WIDTH), w_out_b, y_p, g_ao,
                        mod_p, False, seq, _tile(seq, 512))

        proj_s, small_s = _in_proj(y_s, g_ap, mod_s, True, 1, w_big, w_small, b_small, ts)
        k_new = proj_s[:, COL_FK:COL_FV].reshape(db, FOX_KV_HEADS, FOX_HEAD_DIM)
        v_new = proj_s[:, COL_FV:COL_GQ].reshape(db, FOX_KV_HEADS, FOX_HEAD_DIM)
        lf_new = small_s[:, :FOX_HEADS]
        o_fox_s = _fox_decode(proj_s[:, :FOX_WIDTH].reshape(db, FOX_HEADS, FOX_HEAD_DIM),
                              jnp.repeat(k_new, FOX_GROUP, axis=1), jnp.repeat(v_new, FOX_GROUP, axis=1),
                              lf_new.reshape(db, FOX_HEADS, 1),
                              cache_k[l].reshape(n_pool, page * FOX_KV_HEADS, FOX_HEAD_DIM),
                              cache_v[l].reshape(n_pool, page * FOX_KV_HEADS, FOX_HEAD_DIM),
                              cache_logf[l].transpose(0, 2, 1), page_table)
        o_gla_s, s_new = _gla_decode(proj_s[:, COL_GQ:COL_GK].reshape(db, 1, GLA_QK_WIDTH),
                                     proj_s[:, COL_GK:COL_GV].reshape(db, 1, GLA_QK_WIDTH),
                                     proj_s[:, COL_GV:COL_GR].reshape(db, 1, GLA_WIDTH),
                                     proj_s[:, COL_GR:].reshape(db, 1, GLA_WIDTH),
                                     small_s.reshape(db, 1, SMALL_COLS), wup, bg, gn, state_gla[l])
        outs[4].append(k_new.reshape(db, 1, FOX_KV_HEADS, FOX_HEAD_DIM))
        outs[5].append(v_new.reshape(db, 1, FOX_KV_HEADS, FOX_HEAD_DIM))
        outs[6].append(lf_new.reshape(db, 1, FOX_HEADS))
        outs[7].append(s_new)
        y_s = _out_proj(o_fox_s.reshape(ts, FOX_WIDTH), o_gla_s.reshape(ts, GLA_WIDTH), w_out_b, y_s, g_ao,
                        mod_s, True, 1, ts)

        t_p, lg_p = _ffn_pre(y_p, g_fp, mod_p, False, seq, w_router[l], _tile(seq, 512))
        t_s, lg_s = _ffn_pre(y_s, g_fp, mod_s, True, 1, w_router[l], ts)
        t_all = jnp.concatenate([t_p, t_s], axis=0)
        idx, wts, rank, counts = _route(jnp.concatenate([lg_p, lg_s], axis=0), b_router[l])
        tm_e = EXPERT_ROW_BLOCK
        dest, slot_tok, slot_w, tables = _dispatch(idx, wts, rank, counts, tm_e)
        x_sorted = t_all.at[slot_tok].get(mode="promise_in_bounds")
        h_sorted = _expert_gate_up(tables, x_sorted, w_e_gate[l:l + 1], w_e_up[l:l + 1], tm_e, tf=512)
        y_sorted = _expert_down(tables, h_sorted, slot_w, w_e_down[l:l + 1], tm_e, tn=4096)
        gathered = y_sorted.at[dest.T].get(mode="promise_in_bounds")
        wsg, wsu, wsd = w_s_gate[l].astype(BF16), w_s_up[l].astype(BF16), w_s_down[l].astype(BF16)
        sh_p = _shared_mlp(t_p, wsg, wsu, wsd, _tile(seq, 512))
        sh_s = _shared_mlp(t_s, wsg, wsu, wsd, ts)
        tm_c = math.gcd(LANES, ts)
        y_p = _combine_final(gathered, 0, sh_p, y_p, g_fo, mod_p, False, seq, tm_c)
        y_s = _combine_final(gathered, tp // tm_c, sh_s, y_s, g_fo, mod_s, True, 1, tm_c)

    k_p, v_p, f_p, s_p, k_s, v_s, f_s, s_s = [jnp.stack(o) for o in outs]
    return (y_p.reshape(bsz, seq, d), y_s.reshape(db, 1, d), k_p, v_p, f_p, s_p, k_s, v_s, f_s, s_s)
```

```python
import functools
import math

import numpy as np
import jax
import jax.numpy as jnp
from jax import lax
from jax.experimental import pallas as pl
from jax.experimental.pallas import tpu as pltpu

F32 = jnp.float32
BF16 = jnp.bfloat16
HI = lax.Precision.HIGHEST
NT_DIMS = (((1,), (1,)), ((), ()))
TN_DIMS = (((0,), (0,)), ((), ()))

FOX_HEADS = 16
FOX_KV_HEADS = 4
FOX_GROUP = FOX_HEADS // FOX_KV_HEADS
FOX_HEAD_DIM = 128
FOX_WIDTH = FOX_HEADS * FOX_HEAD_DIM
FOX_KV_WIDTH = FOX_KV_HEADS * FOX_HEAD_DIM
GLA_HEADS = 4
GLA_DK = 256
GLA_DV = 512
GLA_QK_WIDTH = GLA_HEADS * GLA_DK
GLA_WIDTH = GLA_HEADS * GLA_DV
GLA_GATE_RANK = 16
GLA_GATE_TEMP = 16.0
GLA_CHUNK = 64
GLA_SUB = 16
N_EXPERTS = 64
TOP_K = 8
N_GROUPS = 8
TOPK_GROUPS = 4
EXPERTS_PER_GROUP = N_EXPERTS // N_GROUPS
ROUTED_SCALE = 2.5
NORM_EPS = 1e-6
EXPERT_ROW_BLOCK = 256
EXPERT_ROW_SUB = 128

LANES = 128
SUBLANES = 8
VMEM_LIMIT_BYTES = 56 * 1024 * 1024

COL_FQ = 0
COL_FK = COL_FQ + FOX_WIDTH
COL_FV = COL_FK + FOX_KV_WIDTH
COL_GQ = COL_FV + FOX_KV_WIDTH
COL_GK = COL_GQ + GLA_QK_WIDTH
COL_GV = COL_GK + GLA_QK_WIDTH
COL_GR = COL_GV + GLA_WIDTH
BIG_COLS = COL_GR + GLA_WIDTH
SMALL_COLS = LANES


def _params(*sem):
    return pltpu.CompilerParams(dimension_semantics=sem, vmem_limit_bytes=VMEM_LIMIT_BYTES)


def _sigmoid(x):
    return 1.0 / (1.0 + jnp.exp(-x))


def _silu(x):
    return x * _sigmoid(x)


def _log_sigmoid(x):
    return jnp.minimum(x, 0.0) - jnp.log(1.0 + jnp.exp(-jnp.abs(x)))


def _rms(x, g):
    return x * lax.rsqrt(jnp.mean(x * x, axis=-1, keepdims=True) + NORM_EPS) * g


def _iota(shape, dim):
    return lax.broadcasted_iota(jnp.int32, shape, dim)


def _mod_kernel(c_ref, w_ref, b_ref, o_ref):
    a = _silu(c_ref[...]).astype(BF16)
    o_ref[...] = jnp.dot(a, w_ref[...].astype(BF16), preferred_element_type=F32) + b_ref[...]


def _modulation(c_all, w_ada, b_ada, tn=512):
    mp, d = c_all.shape
    n = w_ada.shape[1]
    return pl.pallas_call(
        _mod_kernel,
        grid=(n // tn,),
        in_specs=[pl.BlockSpec((mp, d), lambda j: (0, 0)),
                  pl.BlockSpec((d, tn), lambda j: (0, j)),
                  pl.BlockSpec((1, tn), lambda j: (0, j))],
        out_specs=pl.BlockSpec((mp, tn), lambda j: (0, j)),
        out_shape=jax.ShapeDtypeStruct((mp, n), F32),
        compiler_params=_params("parallel"),
        name="modulation",
    )(c_all, w_ada, b_ada.reshape(1, n))


def _mod_spec(per_row, which, tm, d, tiles_per_seq):
    if per_row:
        return pl.BlockSpec((None, tm, d), lambda i, *_: (which, i, 0))
    return pl.BlockSpec((None, None, 1, d), lambda i, *_: (i // tiles_per_seq, which, 0, 0))


def _inproj_kernel(x_ref, g_ref, sh_ref, sc_ref, w_ref, ws_ref, bs_ref, o_ref, os_ref, h_ref):
    @pl.when(pl.program_id(1) == 0)
    def _():
        h = _rms(x_ref[...], g_ref[...]) * (1.0 + sc_ref[...]) + sh_ref[...]
        hb = h.astype(BF16)
        h_ref[...] = hb
        sm = jnp.dot(hb, ws_ref[...], preferred_element_type=F32) + bs_ref[...]
        os_ref[...] = jnp.where(_iota(sm.shape, 1) < FOX_HEADS, _log_sigmoid(sm), sm)

    o_ref[...] = jnp.dot(h_ref[...], w_ref[...], preferred_element_type=F32)


def _in_proj(x, g, mod, per_row, rows_per_seq, w_big, w_small, b_small, tm, tn=512):
    t, d = x.shape
    tps = max(rows_per_seq // tm, 1)
    return pl.pallas_call(
        _inproj_kernel,
        grid=(t // tm, BIG_COLS // tn),
        in_specs=[pl.BlockSpec((tm, d), lambda i, j: (i, 0), pipeline_mode=pl.Buffered(1)),
                  pl.BlockSpec((1, d), lambda i, j: (0, 0)),
                  _mod_spec(per_row, 0, tm, d, tps),
                  _mod_spec(per_row, 1, tm, d, tps),
                  pl.BlockSpec((d, tn), lambda i, j: (0, j)),
                  pl.BlockSpec((d, SMALL_COLS), lambda i, j: (0, 0)),
                  pl.BlockSpec((1, SMALL_COLS), lambda i, j: (0, 0))],
        out_specs=[pl.BlockSpec((tm, tn), lambda i, j: (i, j)),
                   pl.BlockSpec((tm, SMALL_COLS), lambda i, j: (i, 0))],
        out_shape=[jax.ShapeDtypeStruct((t, BIG_COLS), F32),
                   jax.ShapeDtypeStruct((t, SMALL_COLS), F32)],
        scratch_shapes=[pltpu.VMEM((tm, d), BF16)],
        compiler_params=_params("parallel", "arbitrary"),
        name="in_proj",
    )(x, g, mod, mod, w_big, w_small, b_small)


def _cum_kernel(x_ref, o_ref, carry_ref):
    @pl.when(pl.program_id(1) == 0)
    def _():
        carry_ref[...] = jnp.zeros_like(carry_ref)

    x = x_ref[...]
    tl = x.shape[0]
    tri = (_iota((tl, tl), 1) <= _iota((tl, tl), 0)).astype(F32)
    cum = jnp.dot(tri, x, precision=HI, preferred_element_type=F32) + carry_ref[...]
    carry_ref[...] = cum[tl - 1:tl, :]
    o_ref[...] = cum.T[:FOX_HEADS, :]


def _cum_logf(small, tl):
    b, l, _ = small.shape
    return pl.pallas_call(
        _cum_kernel,
        grid=(b, l // tl),
        in_specs=[pl.BlockSpec((None, tl, SMALL_COLS), lambda i, t: (i, t, 0))],
        out_specs=pl.BlockSpec((None, FOX_HEADS, tl), lambda i, t: (i, 0, t)),
        out_shape=jax.ShapeDtypeStruct((b, FOX_HEADS, l), F32),
        scratch_shapes=[pltpu.VMEM((1, SMALL_COLS), F32)],
        compiler_params=_params("parallel", "arbitrary"),
        name="cum_logf",
    )(small)


def _fox_kernel(qi_tab, kj_tab, q_ref, k_ref, v_ref, ck_ref, o_ref, qs_ref, m_ref, l_ref, acc_ref, *, tq):
    p_id = pl.program_id(2)
    qi = qi_tab[p_id]
    kj = kj_tab[p_id]
    scale = FOX_HEAD_DIM ** -0.5

    @pl.when(kj == 0)
    def _():
        m_ref[...] = jnp.full_like(m_ref, -jnp.inf)
        l_ref[...] = jnp.zeros_like(l_ref)
        acc_ref[...] = jnp.zeros_like(acc_ref)
        for g in range(FOX_GROUP):
            qs_ref[g * tq:(g + 1) * tq, :] = (q_ref[:, g * FOX_HEAD_DIM:(g + 1) * FOX_HEAD_DIM] * scale).astype(BF16)

    def step(masked):
        kb = k_ref[...].astype(BF16)
        vb = v_ref[...].astype(BF16)
        s = lax.dot_general(qs_ref[...], kb, NT_DIMS, preferred_element_type=F32)
        ck = ck_ref[...]
        tk = kb.shape[0]
        parts = []
        for g in range(FOX_GROUP):
            sg = s[g * tq:(g + 1) * tq, :] - ck[g:g + 1, :]
            if masked:
                sg = jnp.where(_iota((tq, tk), 1) <= _iota((tq, tk), 0), sg, -jnp.inf)
            parts.append(sg)
        s = jnp.concatenate(parts, axis=0)
        m_prev = m_ref[...]
        m_next = jnp.maximum(m_prev, jnp.max(s, axis=1, keepdims=True))
        p = jnp.exp(s - m_next[:, :1])
        alpha = jnp.exp(m_prev - m_next)
        l_ref[...] = alpha * l_ref[...] + jnp.sum(p, axis=1, keepdims=True)
        m_ref[...] = m_next
        acc_ref[...] = alpha * acc_ref[...] + jnp.dot(p.astype(BF16), vb, preferred_element_type=F32)

    @pl.when(kj < qi)
    def _():
        step(False)

    @pl.when(kj == qi)
    def _():
        step(True)
        o = acc_ref[...] / l_ref[...]
        for g in range(FOX_GROUP):
            o_ref[:, g * FOX_HEAD_DIM:(g + 1) * FOX_HEAD_DIM] = o[g * tq:(g + 1) * tq, :].astype(o_ref.dtype)


def _fox_prompt(proj, cum4, tq):
    b, l, _ = proj.shape
    nq = l // tq
    pairs = [(i, j) for i in range(nq) for j in range(i + 1)]
    qi_tab = jnp.asarray(np.array([p[0] for p in pairs], np.int32))
    kj_tab = jnp.asarray(np.array([p[1] for p in pairs], np.int32))
    gw = FOX_GROUP * FOX_HEAD_DIM
    kcol = COL_FK // FOX_HEAD_DIM
    vcol = COL_FV // FOX_HEAD_DIM
    grid_spec = pltpu.PrefetchScalarGridSpec(
        num_scalar_prefetch=2,
        grid=(b, FOX_KV_HEADS, len(pairs)),
        in_specs=[pl.BlockSpec((None, tq, gw), lambda i, h, p, qt, kt: (i, qt[p], h)),
                  pl.BlockSpec((None, tq, FOX_HEAD_DIM), lambda i, h, p, qt, kt: (i, kt[p], kcol + h)),
                  pl.BlockSpec((None, tq, FOX_HEAD_DIM), lambda i, h, p, qt, kt: (i, kt[p], vcol + h)),
                  pl.BlockSpec((None, None, FOX_GROUP, tq), lambda i, h, p, qt, kt: (i, h, 0, kt[p]))],
        out_specs=pl.BlockSpec((None, tq, gw), lambda i, h, p, qt, kt: (i, qt[p], h)),
        scratch_shapes=[pltpu.VMEM((FOX_GROUP * tq, FOX_HEAD_DIM), BF16),
                        pltpu.VMEM((FOX_GROUP * tq, FOX_HEAD_DIM), F32),
                        pltpu.VMEM((FOX_GROUP * tq, FOX_HEAD_DIM), F32),
                        pltpu.VMEM((FOX_GROUP * tq, FOX_HEAD_DIM), F32)],
    )
    return pl.pallas_call(
        functools.partial(_fox_kernel, tq=tq),
        grid_spec=grid_spec,
        out_shape=jax.ShapeDtypeStruct((b, l, FOX_WIDTH), BF16),
        compiler_params=_params("parallel", "parallel", "arbitrary"),
        name="fox_prompt",
    )(qi_tab, kj_tab, proj, proj, proj, cum4)


def _foxdec_kernel(pt_ref, q_ref, kn_ref, vn_ref, lfn_ref, *refs, n_pages):
    k_refs = refs[:n_pages]
    v_refs = refs[n_pages:2 * n_pages]
    lf_refs = refs[2 * n_pages:3 * n_pages]
    o_ref = refs[3 * n_pages]
    del pt_ref
    scale = FOX_HEAD_DIM ** -0.5
    cols = k_refs[0].shape[0]
    page = cols // FOX_KV_HEADS
    q = q_ref[...] * scale
    qb = q.astype(BF16)
    valid = (_iota((FOX_HEADS, cols), 1) % FOX_KV_HEADS) == (_iota((FOX_HEADS, cols), 0) // FOX_GROUP)
    spread = (_iota((page, cols), 0) <= _iota((page, cols), 1) // FOX_KV_HEADS).astype(F32)

    lfs = [lf_refs[p][...] for p in range(n_pages)]
    cums = jnp.dot(jnp.concatenate(lfs, axis=0), spread, precision=HI, preferred_element_type=F32)
    off = jnp.zeros((FOX_HEADS, 1), F32)
    logits = []
    for p in range(n_pages):
        cum = cums[p * FOX_HEADS:(p + 1) * FOX_HEADS, :] + off
        off = off + jnp.sum(lfs[p], axis=1, keepdims=True)
        sp = lax.dot_general(qb, k_refs[p][...].astype(BF16), NT_DIMS, preferred_element_type=F32)
        logits.append(jnp.where(valid, sp - cum, -jnp.inf))
    s_new = jnp.sum(q * kn_ref[...], axis=1, keepdims=True) - (off + lfn_ref[...])

    m = s_new
    for lg in logits:
        m = jnp.maximum(m, jnp.max(lg, axis=1, keepdims=True))
    p_new = jnp.exp(s_new - m)
    l = p_new
    acc = p_new * vn_ref[...]
    for p in range(n_pages):
        pp = jnp.exp(logits[p] - m)
        l = l + jnp.sum(pp, axis=1, keepdims=True)
        acc = acc + jnp.dot(pp.astype(BF16), v_refs[p][...].astype(BF16), preferred_element_type=F32)
    o_ref[...] = (acc / l).astype(o_ref.dtype)


def _fox_decode(q, k_new, v_new, lf_new, cache_k, cache_v, cache_lft, page_table):
    db, n_pages = page_table.shape

    def page_spec(p, shape):
        return pl.BlockSpec((None,) + shape, lambda i, pt: (pt[i, p], 0, 0))

    row = pl.BlockSpec((None, FOX_HEADS, FOX_HEAD_DIM), lambda i, pt: (i, 0, 0))
    in_specs = [row, row, row, pl.BlockSpec((None, FOX_HEADS, 1), lambda i, pt: (i, 0, 0))]
    in_specs += [page_spec(p, cache_k.shape[1:]) for p in range(n_pages)]
    in_specs += [page_spec(p, cache_v.shape[1:]) for p in range(n_pages)]
    in_specs += [page_spec(p, cache_lft.shape[1:]) for p in range(n_pages)]
    grid_spec = pltpu.PrefetchScalarGridSpec(
        num_scalar_prefetch=1, grid=(db,), in_specs=in_specs,
        out_specs=pl.BlockSpec((None, FOX_HEADS, FOX_HEAD_DIM), lambda i, pt: (i, 0, 0)))
    return pl.pallas_call(
        functools.partial(_foxdec_kernel, n_pages=n_pages),
        grid_spec=grid_spec,
        out_shape=jax.ShapeDtypeStruct((db, FOX_HEADS, FOX_HEAD_DIM), BF16),
        compiler_params=_params("parallel"),
        name="fox_decode",
    )(page_table, q, k_new, v_new, lf_new, *([cache_k] * n_pages), *([cache_v] * n_pages),
      *([cache_lft] * n_pages))


def _gla_gate(gg, wup, bg):
    return _log_sigmoid(jnp.dot(gg, wup, precision=HI, preferred_element_type=F32) + bg) / GLA_GATE_TEMP


def _gla_out(o, g, gr):
    return _rms(o, g) * _silu(gr)


def _gla_head(q, k, v, r, la, gn, s_old):
    c = q.shape[0]
    tri = (_iota((c, c), 1) <= _iota((c, c), 0)).astype(F32)
    b = jnp.dot(tri, la, precision=HI, preferred_element_type=F32)
    b_last = b[c - 1:c, :]
    vb = v.astype(BF16)
    o = jnp.dot((q * jnp.exp(b)).astype(BF16), s_old.astype(BF16), preferred_element_type=F32)

    sb = min(GLA_SUB, c)
    rows_c = _iota((c, GLA_DK), 0)
    rows_s = _iota((sb, GLA_DK), 0)
    lane = _iota((sb, c), 1)
    a_blocks = []
    for i0 in range(0, c, sb):
        qi = q[i0:i0 + sb, :]
        bi = b[i0:i0 + sb, :]
        a = jnp.zeros((sb, c), F32)
        if i0 > 0:
            rr = b[i0:i0 + 1, :]
            qt = (qi * jnp.exp(bi - rr)).astype(BF16)
            kt = (k * jnp.exp(jnp.where(rows_c < i0, rr - b, -jnp.inf))).astype(BF16)
            a = lax.dot_general(qt, kt, NT_DIMS, preferred_element_type=F32)
        for jj in range(sb):
            j = i0 + jj
            w = jnp.exp(jnp.where(rows_s >= jj, bi - b[j:j + 1, :], -jnp.inf))
            col = jnp.sum(qi * w * k[j:j + 1, :], axis=1, keepdims=True)
            a = jnp.where(lane == j, col, a)
        a_blocks.append(a)
    a = jnp.concatenate(a_blocks, axis=0)
    o = o + jnp.dot(a.astype(BF16), vb, preferred_element_type=F32)

    kd = (k * jnp.exp(b_last - b)).astype(BF16)
    ones = jnp.ones((c, LANES), F32)
    decay = jnp.exp(lax.dot_general(la, ones, TN_DIMS, precision=HI, preferred_element_type=F32))
    decay = jnp.concatenate([decay] * (GLA_DV // LANES), axis=1)
    s_new = decay * s_old + lax.dot_general(kd, vb, TN_DIMS, preferred_element_type=F32)
    return _gla_out(o, gn, r), s_new


def _gla_kernel(q_ref, k_ref, *refs):
    v_refs = refs[:GLA_HEADS]
    r_refs = refs[GLA_HEADS:2 * GLA_HEADS]
    sm_ref, wup_ref, bg_ref, gn_ref, o_ref, s_ref = refs[2 * GLA_HEADS:]

    @pl.when(pl.program_id(1) == 0)
    def _():
        s_ref[...] = jnp.zeros_like(s_ref)

    gg = sm_ref[:, FOX_HEADS:FOX_HEADS + GLA_GATE_RANK]
    la = _gla_gate(gg, wup_ref[...], bg_ref[...])
    for h in range(GLA_HEADS):
        ks = slice(h * GLA_DK, (h + 1) * GLA_DK)
        o, s_new = _gla_head(q_ref[:, ks] * (GLA_DK ** -0.5), k_ref[:, ks], v_refs[h][...], r_refs[h][...],
                             la[:, ks], gn_ref[...], s_ref[h])
        o_ref[:, h * GLA_DV:(h + 1) * GLA_DV] = o.astype(o_ref.dtype)
        s_ref[h] = s_new


def _gla_prompt(proj, small, wup, bg, gn):
    b, l, _ = proj.shape
    c = GLA_CHUNK if l % GLA_CHUNK == 0 else l
    qc, kc = COL_GQ // GLA_QK_WIDTH, COL_GK // GLA_QK_WIDTH
    vc, rc = COL_GV // GLA_DV, COL_GR // GLA_DV

    def head_specs(col0):
        return [pl.BlockSpec((None, c, GLA_DV), functools.partial(lambda i, t, h: (i, t, col0 + h), h=h))
                for h in range(GLA_HEADS)]

    return pl.pallas_call(
        _gla_kernel,
        grid=(b, l // c),
        in_specs=[pl.BlockSpec((None, c, GLA_QK_WIDTH), lambda i, t: (i, t, qc)),
                  pl.BlockSpec((None, c, GLA_QK_WIDTH), lambda i, t: (i, t, kc))]
        + head_specs(vc) + head_specs(rc)
        + [pl.BlockSpec((None, c, SMALL_COLS), lambda i, t: (i, t, 0)),
           pl.BlockSpec((GLA_GATE_RANK, GLA_QK_WIDTH), lambda i, t: (0, 0)),
           pl.BlockSpec((1, GLA_QK_WIDTH), lambda i, t: (0, 0)),
           pl.BlockSpec((1, GLA_DV), lambda i, t: (0, 0))],
        out_specs=[pl.BlockSpec((None, c, GLA_WIDTH), lambda i, t: (i, t, 0)),
                   pl.BlockSpec((None, GLA_HEADS, GLA_DK, GLA_DV), lambda i, t: (i, 0, 0, 0))],
        out_shape=[jax.ShapeDtypeStruct((b, l, GLA_WIDTH), BF16),
                   jax.ShapeDtypeStruct((b, GLA_HEADS, GLA_DK, GLA_DV), F32)],
        compiler_params=_params("parallel", "arbitrary"),
        name="gla_prompt",
    )(proj, proj, *([proj] * (2 * GLA_HEADS)), small, wup, bg, gn)


def _gladec_kernel(q_ref, k_ref, v_ref, r_ref, sm_ref, wup_ref, bg_ref, gn_ref, s_ref, o_ref, so_ref):
    gg = sm_ref[:, FOX_HEADS:FOX_HEADS + GLA_GATE_RANK]
    la = _gla_gate(gg, wup_ref[...], bg_ref[...])
    eye = (_iota((GLA_DK, GLA_DK), 0) == _iota((GLA_DK, GLA_DK), 1)).astype(F32)
    for h in range(GLA_HEADS):
        ks = slice(h * GLA_DK, (h + 1) * GLA_DK)
        vs = slice(h * GLA_DV, (h + 1) * GLA_DV)
        rows3 = jnp.concatenate([jnp.exp(la[:, ks]), k_ref[:, ks], q_ref[:, ks] * (GLA_DK ** -0.5),
                                 jnp.zeros((SUBLANES - 3, GLA_DK), F32)], axis=0)
        cols = lax.dot_general(eye, rows3, NT_DIMS, precision=HI, preferred_element_type=F32)
        s_new = cols[:, 0:1] * s_ref[h] + cols[:, 1:2] * v_ref[:, vs]
        so_ref[h] = s_new
        o = jnp.sum(cols[:, 2:3] * s_new, axis=0, keepdims=True)
        o_ref[:, vs] = _gla_out(o, gn_ref[...], r_ref[:, vs]).astype(o_ref.dtype)


def _gla_decode(gq, gk, gv, gr, small3, wup, bg, gn, state):
    db = gq.shape[0]
    st =pl.BlockSpec((None, GLA_HEADS, GLA_DK, GLA_DV), lambda i: (i, 0, 0, 0))
    return pl.pallas_call(
        _gladec_kernel,
        grid=(db,),
        in_specs=[pl.BlockSpec((None, 1, GLA_QK_WIDTH), lambda i: (i, 0, 0)),
                  pl.BlockSpec((None, 1, GLA_QK_WIDTH), lambda i: (i, 0, 0)),
                  pl.BlockSpec((None, 1, GLA_WIDTH), lambda i: (i, 0, 0)),
                  pl.BlockSpec((None, 1, GLA_WIDTH), lambda i: (i, 0, 0)),
                  pl.BlockSpec((None, 1, SMALL_COLS), lambda i: (i, 0, 0)),
                  pl.BlockSpec((GLA_GATE_RANK, GLA_QK_WIDTH), lambda i: (0, 0)),
                  pl.BlockSpec((1, GLA_QK_WIDTH), lambda i: (0, 0)),
                  pl.BlockSpec((1, GLA_DV), lambda i: (0, 0)),
                  st],
        out_specs=[pl.BlockSpec((None, 1, GLA_WIDTH), lambda i: (i, 0, 0)), st],
        out_shape=[jax.ShapeDtypeStruct((db, 1, GLA_WIDTH), BF16),
                   jax.ShapeDtypeStruct(state.shape, F32)],
        compiler_params=_params("parallel"),
        name="gla_decode",
    )(gq, gk, gv, gr, small3, wup, bg, gn, state)


def _outproj_kernel(a_ref, b_ref, wa_ref, wb_ref, x_ref, g_ref, gt_ref, o_ref, acc_ref):
    j = pl.program_id(1)
    nj = pl.num_programs(1)
    acc_ref[j] = (jnp.dot(a_ref[...], wa_ref[...], preferred_element_type=F32)
                  + jnp.dot(b_ref[...], wb_ref[...], preferred_element_type=F32))

    @pl.when(j == nj - 1)
    def _():
        n_chunks, _, tn = acc_ref.shape
        ss = jnp.zeros((acc_ref.shape[1], 1), F32)
        for c in range(n_chunks):
            z = acc_ref[c]
            ss = ss + jnp.sum(z * z, axis=1, keepdims=True)
        rs = lax.rsqrt(ss / (n_chunks * tn) + NORM_EPS)
        for c in range(n_chunks):
            cs = slice(c * tn, (c + 1) * tn)
            o_ref[:, cs] = x_ref[:, cs] + gt_ref[:, cs] * (acc_ref[c] * rs * g_ref[:, cs])


def _out_proj(o_fox, o_gla, w_out, x, g, mod, per_row, rows_per_seq, tm, tn=512):
    t, d = x.shape
    ka = o_fox.shape[1]
    tn = min(tn, d)
    tps = max(rows_per_seq // tm, 1)
    return pl.pallas_call(
        _outproj_kernel,
        grid=(t // tm, d // tn),
        in_specs=[pl.BlockSpec((tm, ka), lambda i, j: (i, 0)),
                  pl.BlockSpec((tm, ka), lambda i, j: (i, 0)),
                  pl.BlockSpec((ka, tn), lambda i, j: (0, j)),
                  pl.BlockSpec((ka, tn), lambda i, j: (1, j)),
                  pl.BlockSpec((tm, d), lambda i, j: (i, 0), pipeline_mode=pl.Buffered(1)),
                  pl.BlockSpec((1, d), lambda i, j: (0, 0)),
                  _mod_spec(per_row, 2, tm, d, tps)],
        out_specs=pl.BlockSpec((tm, d), lambda i, j: (i, 0)),
        out_shape=jax.ShapeDtypeStruct((t, d), F32),
        scratch_shapes=[pltpu.VMEM((d // tn, tm, tn), F32)],
        compiler_params=_params("parallel", "arbitrary"),
        name="out_proj",
    )(o_fox, o_gla, w_out, w_out, x, g, mod)


def _ffnpre_kernel(x_ref, g_ref, sh_ref, sc_ref, wr_ref, t_ref, lg_ref):
    h = _rms(x_ref[...], g_ref[...]) * (1.0 + sc_ref[...]) + sh_ref[...]
    t_ref[...] = h.astype(BF16)
    lg_ref[...] = jnp.dot(h, wr_ref[...], precision=HI, preferred_element_type=F32)


def _ffn_pre(x, g, mod, per_row, rows_per_seq, w_router, tm):
    t, d = x.shape
    tps = max(rows_per_seq // tm, 1)
    return pl.pallas_call(
        _ffnpre_kernel,
        grid=(t // tm,),
        in_specs=[pl.BlockSpec((tm, d), lambda i: (i, 0)),
                  pl.BlockSpec((1, d), lambda i: (0, 0)),
                  _mod_spec(per_row, 3, tm, d, tps),
                  _mod_spec(per_row, 4, tm, d, tps),
                  pl.BlockSpec((d, N_EXPERTS), lambda i: (0, 0))],
        out_specs=[pl.BlockSpec((tm, d), lambda i: (i, 0)),
                   pl.BlockSpec((tm, N_EXPERTS), lambda i: (i, 0))],
        out_shape=[jax.ShapeDtypeStruct((t, d), BF16),
                   jax.ShapeDtypeStruct((t, N_EXPERTS), F32)],
        compiler_params=_params("parallel"),
        name="ffn_pre",
    )(x, g, mod, mod, w_router)


def _new_expert(be_ref, i):
    return jnp.logical_or(i == 0, be_ref[i] != be_ref[jnp.maximum(i - 1, 0)])


def _stream_weights(be_ref, nu_ref, run_ref, rexp_ref, nr_ref, copies, cast):
    p = pl.program_id(0)
    i = pl.program_id(1)

    @pl.when(jnp.logical_and(i < nu_ref[0], _new_expert(be_ref, i)))
    def _():
        run = run_ref[i]
        last = run + 1 == nr_ref[0]

        @pl.when(jnp.logical_and(p == 0, i == 0))
        def _():
            for c in copies(p, be_ref[i]):
                c.start()

        for c in copies(p, be_ref[i]):
            c.wait()
        cast()

        @pl.when(jnp.logical_or(jnp.logical_not(last), p + 1 < pl.num_programs(0)))
        def _():
            for c in copies(jnp.where(last, p + 1, p), rexp_ref[jnp.where(last, 0, run + 1)]):
                c.start()


def _by_valid_rows(nv, out_ref, compute):
    tm = out_ref.shape[0]
    sub = min(EXPERT_ROW_SUB, tm)
    for n in range(sub, tm + 1, sub):
        @pl.when(jnp.logical_and(nv > n - sub, nv <= n))
        def _(n=n):
            out_ref[:n, :] = compute(n)
            if n < tm:
                out_ref[n:, :] = jnp.zeros((tm - n, out_ref.shape[1]), out_ref.dtype)

    @pl.when(nv <= 0)
    def _():
        out_ref[...] = jnp.zeros_like(out_ref)


def _gateup_kernel(be_ref, nu_ref, run_ref, rexp_ref, nr_ref, nv_ref, x_ref, wg_hbm, wu_hbm, h_ref,
                   wgf_ref, wuf_ref, wgb_ref, wub_ref, sem):
    tf = wgf_ref.shape[1]

    def copies(p, e):
        cols = pl.ds(pl.multiple_of(p * tf, tf), tf)
        return (pltpu.make_async_copy(wg_hbm.at[0, e, :, cols], wgf_ref, sem.at[0]),
                pltpu.make_async_copy(wu_hbm.at[0, e, :, cols], wuf_ref, sem.at[1]))

    def cast():
        wgb_ref[...] = wgf_ref[...].astype(BF16)
        wub_ref[...] = wuf_ref[...].astype(BF16)

    _stream_weights(be_ref, nu_ref, run_ref, rexp_ref, nr_ref, copies, cast)

    def compute(n):
        x = x_ref[:n, :]
        g = jnp.dot(x, wgb_ref[...], preferred_element_type=F32)
        u = jnp.dot(x, wub_ref[...], preferred_element_type=F32)
        return (_silu(g) * u).astype(h_ref.dtype)

    _by_valid_rows(nv_ref[pl.program_id(1)], h_ref, compute)


def _expert_gate_up(tables, x_sorted, w_gate, w_up, tm, tf):
    r, d = x_sorted.shape
    ff = w_gate.shape[-1]
    tf = min(tf, ff)
    grid_spec = pltpu.PrefetchScalarGridSpec(
        num_scalar_prefetch=len(tables), grid=(ff // tf, r // tm),
        in_specs=[pl.BlockSpec((tm, d), lambda p, i, *_: (i, 0)),
                  pl.BlockSpec(memory_space=pl.ANY), pl.BlockSpec(memory_space=pl.ANY)],
        out_specs=pl.BlockSpec((tm, tf), lambda p, i, *_: (i, p)),
        scratch_shapes=[pltpu.VMEM((d, tf), F32), pltpu.VMEM((d, tf), F32),
                        pltpu.VMEM((d, tf), BF16), pltpu.VMEM((d, tf), BF16),
                        pltpu.SemaphoreType.DMA((2,))])
    return pl.pallas_call(
        _gateup_kernel, grid_spec=grid_spec,
        out_shape=jax.ShapeDtypeStruct((r, ff), BF16),
        compiler_params=_params("arbitrary", "arbitrary"),
        name="expert_gate_up",
    )(*tables, x_sorted, w_gate, w_up)


def _down_kernel(be_ref, nu_ref, run_ref, rexp_ref, nr_ref, nv_ref, h_ref, sw_ref, wd_hbm, y_ref, wdf_ref, wdb_ref,
                 sem):
    tn = wdf_ref.shape[1]

    def copies(p, e):
        cols = pl.ds(pl.multiple_of(p * tn, tn), tn)
        return (pltpu.make_async_copy(wd_hbm.at[0, e, :, cols], wdf_ref, sem.at[0]),)

    def cast():
        wdb_ref[...] = wdf_ref[...].astype(BF16)

    _stream_weights(be_ref, nu_ref, run_ref, rexp_ref, nr_ref, copies, cast)

    def compute(n):
        y = jnp.dot(h_ref[:n, :], wdb_ref[...], preferred_element_type=F32)
        return (y * sw_ref[:n, :]).astype(y_ref.dtype)

    _by_valid_rows(nv_ref[pl.program_id(1)], y_ref, compute)


def _expert_down(tables, h_sorted, slot_w, w_down, tm, tn):
    r, ff = h_sorted.shape
    d = w_down.shape[-1]
    tn = min(tn, d)
    grid_spec = pltpu.PrefetchScalarGridSpec(
        num_scalar_prefetch=len(tables), grid=(d // tn, r // tm),
        in_specs=[pl.BlockSpec((tm, ff), lambda p, i, *_: (i, 0)),
                  pl.BlockSpec((tm, 1), lambda p, i, *_: (i, 0)),
                  pl.BlockSpec(memory_space=pl.ANY)],
        out_specs=pl.BlockSpec((tm, tn), lambda p, i, *_: (i, p)),
        scratch_shapes=[pltpu.VMEM((ff, tn), F32), pltpu.VMEM((ff, tn), BF16), pltpu.SemaphoreType.DMA((1,))])
    return pl.pallas_call(
        _down_kernel, grid_spec=grid_spec,
        out_shape=jax.ShapeDtypeStruct((r, d), BF16),
        compiler_params=_params("arbitrary", "arbitrary"),
        name="expert_down",
    )(*tables, h_sorted, slot_w, w_down)


def _shared_kernel(t_ref, wg_ref, wu_ref, wd_ref, o_ref):
    f = pl.program_id(1)
    t = t_ref[...]
    h = _silu(jnp.dot(t, wg_ref[...], preferred_element_type=F32)) * jnp.dot(t, wu_ref[...], preferred_element_type=F32)
    y = jnp.dot(h.astype(BF16), wd_ref[...], preferred_element_type=F32)

    @pl.when(f == 0)
    def _():
        o_ref[...] = y

    @pl.when(f > 0)
    def _():
        o_ref[...] += y


def _shared_mlp(t, wg, wu, wd, tm, tf=256):
    n, d = t.shape
    ff = wg.shape[1]
    return pl.pallas_call(
        _shared_kernel,
        grid=(n // tm, ff // tf),
        in_specs=[pl.BlockSpec((tm, d), lambda i, f: (i, 0)),
                  pl.BlockSpec((d, tf), lambda i, f: (0, f)),
                  pl.BlockSpec((d, tf), lambda i, f: (0, f)),
                  pl.BlockSpec((tf, d), lambda i, f: (f, 0))],
        out_specs=pl.BlockSpec((tm, d), lambda i, f: (i, 0)),
        out_shape=jax.ShapeDtypeStruct((n, d), F32),
        compiler_params=_params("parallel", "arbitrary"),
        name="shared_mlp",
    )(t, wg, wu, wd)


def _combine_kernel(y_ref, sh_ref, x_ref, g_ref, gt_ref, o_ref):
    z = sh_ref[...]
    for k in range(y_ref.shape[0]):
        z = z + y_ref[k].astype(F32)
    o_ref[...] = x_ref[...] + gt_ref[...] * _rms(z, g_ref[...])


def _combine_final(gathered, row_block0, shared, x, g, mod, per_row, rows_per_seq, tm):
    n, d = x.shape
    tps = max(rows_per_seq // tm, 1)
    return pl.pallas_call(
        _combine_kernel,
        grid=(n // tm,),
        in_specs=[pl.BlockSpec((TOP_K, tm, d), lambda i: (0, row_block0 + i, 0)),
                  pl.BlockSpec((tm, d), lambda i: (i, 0)),
                  pl.BlockSpec((tm, d), lambda i: (i, 0)),
                  pl.BlockSpec((1, d), lambda i: (0, 0)),
                  _mod_spec(per_row, 5, tm, d, tps)],
        out_specs=pl.BlockSpec((tm, d), lambda i: (i, 0)),
        out_shape=jax.ShapeDtypeStruct((n, d), F32),
        compiler_params=_params("parallel"),
        name="combine_final",
    )(gathered, shared, x, g, mod)


def _first_max(x, lane, n):
    m = jnp.max(x, axis=1, keepdims=True)
    return m, jnp.min(jnp.where(x == m, lane, n), axis=1, keepdims=True)


def _route_kernel(lg_ref, br_ref, idx_ref, w_ref, rank_ref, cnt_ref, carry_ref):
    @pl.when(pl.program_id(0) == 0)
    def _():
        carry_ref[...] = jnp.zeros_like(carry_ref)

    tm = lg_ref.shape[0]
    scores = _sigmoid(lg_ref[...])
    sel = scores + br_ref[...]
    lane = _iota((tm, N_EXPERTS), 1).astype(F32)
    grp = (_iota((tm, N_EXPERTS), 1) // EXPERTS_PER_GROUP).astype(F32)
    neg = -jnp.inf

    gsc = jnp.full((tm, N_EXPERTS), neg, F32)
    for g in range(N_GROUPS):
        mg = jnp.where(grp == g, sel, neg)
        m1, i1 = _first_max(mg, lane, N_EXPERTS)
        m2 = jnp.max(jnp.where(lane == i1, neg, mg), axis=1, keepdims=True)
        gsc = jnp.where(lane == g, m1 + m2, gsc)
    cand = jnp.full((tm, N_EXPERTS), neg, F32)
    for _ in range(TOPK_GROUPS):
        _, gi = _first_max(gsc, lane, N_EXPERTS)
        cand = jnp.where(grp == gi, sel, cand)
        gsc = jnp.where(lane == gi, neg, gsc)

    col = _iota((tm, TOP_K), 1)
    idx = jnp.zeros((tm, TOP_K), F32)
    wts = jnp.zeros((tm, TOP_K), F32)
    chosen = jnp.zeros((tm, N_EXPERTS), F32)
    hits = []
    for k in range(TOP_K):
        _, ik = _first_max(cand, lane, N_EXPERTS)
        hit = lane == ik
        hits.append(hit)
        cand = jnp.where(hit, neg, cand)
        chosen = jnp.where(hit, 1.0, chosen)
        idx = jnp.where(col == k, ik, idx)
        wts = jnp.where(col == k, jnp.sum(jnp.where(hit, scores, 0.0), axis=1, keepdims=True), wts)
    idx_ref[...] = idx.astype(jnp.int32)
    w_ref[...] = wts / jnp.sum(wts, axis=1, keepdims=True) * ROUTED_SCALE

    below = (_iota((tm, tm), 1) < _iota((tm, tm), 0)).astype(BF16)
    prefix = jnp.dot(below, chosen.astype(BF16), preferred_element_type=F32) + carry_ref[...]
    rank = jnp.zeros((tm, TOP_K), F32)
    for k in range(TOP_K):
        rank = jnp.where(col == k, jnp.sum(jnp.where(hits[k], prefix, 0.0), axis=1, keepdims=True), rank)
    rank_ref[...] = rank.astype(jnp.int32)
    carry_ref[...] += jnp.sum(chosen, axis=0, keepdims=True)
    cnt_ref[...] = carry_ref[...]


def _route(logits, b_router):
    t = logits.shape[0]
    tm = max(m for m in range(SUBLANES, 1025, SUBLANES) if t % m == 0)
    kspec = pl.BlockSpec((tm, TOP_K), lambda i: (i, 0))
    return pl.pallas_call(
        _route_kernel,
        grid=(t // tm,),
        in_specs=[pl.BlockSpec((tm, N_EXPERTS), lambda i: (i, 0)),
                  pl.BlockSpec((1, N_EXPERTS), lambda i: (0, 0))],
        out_specs=[kspec, kspec, kspec, pl.BlockSpec((1, N_EXPERTS), lambda i: (0, 0))],
        out_shape=[jax.ShapeDtypeStruct((t, TOP_K), jnp.int32),
                   jax.ShapeDtypeStruct((t, TOP_K), F32),
                   jax.ShapeDtypeStruct((t, TOP_K), jnp.int32),
                   jax.ShapeDtypeStruct((1, N_EXPERTS), F32)],
        scratch_shapes=[pltpu.VMEM((1, N_EXPERTS), F32)],
        compiler_params=_params("arbitrary"),
        name="route",
    )(logits, b_router.reshape(1, N_EXPERTS))


def _dispatch(idx, wts, rank, counts, tm):
    t = idx.shape[0]
    m = t * TOP_K
    counts = counts.reshape(N_EXPERTS).astype(jnp.int32)
    padded = (counts + tm - 1) // tm * tm
    pad_end = jnp.cumsum(padded)
    pad_start = pad_end - padded
    dest = pad_start[idx] + rank
    n_blocks = -(-(m + N_EXPERTS * (tm - 1)) // tm)
    pairs = jnp.stack([jnp.repeat(jnp.arange(t, dtype=jnp.int32), TOP_K),
                       lax.bitcast_convert_type(wts.reshape(-1), jnp.int32)], axis=-1)
    n_slots = n_blocks * tm
    init = jnp.stack([jnp.arange(n_slots, dtype=jnp.int32) % t, jnp.zeros((n_slots,), jnp.int32)], axis=-1)
    slots = init.at[dest.reshape(-1)].set(pairs, mode="promise_in_bounds", unique_indices=True)
    slot_tok = slots[:, 0]
    slot_w = lax.bitcast_convert_type(slots[:, 1], F32).reshape(-1, 1)
    starts = jnp.arange(n_blocks, dtype=jnp.int32) * tm
    block_expert = jnp.minimum(jnp.sum((pad_end[None, :] <= starts[:, None]).astype(jnp.int32), axis=1), N_EXPERTS - 1)
    n_used = (pad_end[-1] // tm).astype(jnp.int32).reshape(1)
    present = counts > 0
    run_of_expert = jnp.cumsum(present.astype(jnp.int32)) - 1
    run_expert = jnp.argsort(jnp.logical_not(present), stable=True).astype(jnp.int32)
    n_runs = jnp.sum(present.astype(jnp.int32)).reshape(1)
    n_valid = jnp.clip(counts[block_expert] - (starts - pad_start[block_expert]), 0, tm)
    n_valid = jnp.where(starts < pad_end[-1], n_valid, 0).astype(jnp.int32)
    tables = (block_expert, n_used, run_of_expert[block_expert], run_expert, n_runs, n_valid)
    return dest, slot_tok, slot_w, tables


def _tile(n, pref):
    return pref if n % pref == 0 else n


def kernel(x_prompt, x_sample, c_prompt, c_sample, cache_k, cache_v, cache_logf, state_gla, page_table, w_ada, b_ada, g_attn_pre, g_attn_post, g_ffn_pre, g_ffn_post, w_in, b_forget, w_gla_gate_up, b_gla_gate, g_gla_out, w_out, w_router, b_router, w_e_gate, w_e_up, w_e_down, w_s_gate, w_s_up, w_s_down):
    bsz, seq, d = x_prompt.shape
    db = x_sample.shape[0]
    depth = w_ada.shape[0]
    n_pool, page = cache_k.shape[1], cache_k.shape[2]
    tp, ts = bsz * seq, db
    y_p = x_prompt.reshape(tp, d)
    y_s = x_sample.reshape(ts, d)
    outs = [[] for _ in range(8)]

    n_mod = bsz + db
    mp = -(-n_mod // SUBLANES) * SUBLANES
    c_all = jnp.concatenate([c_prompt, c_sample, jnp.zeros((mp - n_mod, d), F32)], axis=0)

    o_fg = FOX_WIDTH + 2 * FOX_KV_WIDTH
    o_gq = o_fg + FOX_HEADS
    o_gg = o_gq + 2 * GLA_QK_WIDTH + GLA_WIDTH
    o_gr = o_gg + GLA_GATE_RANK

    for l in range(depth):
        mod = _modulation(c_all, w_ada[l], b_ada[l])
        mod_p = mod[:bsz].reshape(bsz, 6, 1, d)
        mod_s = mod[bsz:n_mod].reshape(db, 6, d).transpose(1, 0, 2)

        wl = w_in[l]
        w_big = jnp.concatenate([wl[:, :o_fg], wl[:, o_gq:o_gg], wl[:, o_gr:]], axis=1).astype(BF16)
        w_small = jnp.concatenate([wl[:, o_fg:o_gq], wl[:, o_gg:o_gr],
                                   jnp.zeros((d, SMALL_COLS - FOX_HEADS - GLA_GATE_RANK), F32)], axis=1).astype(BF16)
        b_small = jnp.concatenate([b_forget[l], jnp.zeros((SMALL_COLS - FOX_HEADS,), F32)]).reshape(1, SMALL_COLS)
        w_out_b = w_out[l].astype(BF16)
        wup = w_gla_gate_up[l]
        bg = b_gla_gate[l].reshape(1, GLA_QK_WIDTH)
        gn = g_gla_out[l].reshape(1, GLA_DV)
        g_ap = g_attn_pre[l].reshape(1, d)
        g_ao = g_attn_post[l].reshape(1, d)
        g_fp = g_ffn_pre[l].reshape(1, d)
        g_fo = g_ffn_post[l].reshape(1, d)

        tm_p = _tile(seq, 1024)
        proj, small = _in_proj(y_p, g_ap, mod_p, False, seq, w_big, w_small, b_small, tm_p)
        proj3 = proj.reshape(bsz, seq, BIG_COLS)
        small3 = small.reshape(bsz, seq, SMALL_COLS)
        cum = _cum_logf(small3, _tile(seq, 512))
        o_fox = _fox_prompt(proj3, cum.reshape(bsz, FOX_KV_HEADS, FOX_GROUP, seq), _tile(seq, 1024))
        o_gla, s_fin = _gla_prompt(proj3, small3, wup, bg, gn)
        outs[0].append(proj3[:, :, COL_FK:COL_FV].reshape(bsz, seq, FOX_KV_HEADS, FOX_HEAD_DIM))
        outs[1].append(proj3[:, :, COL_FV:COL_GQ].reshape(bsz, seq, FOX_KV_HEADS, FOX_HEAD_DIM))
        outs[2].append(small3[:, :, :FOX_HEADS])
        outs[3].append(s_fin)
        y_p = _out_proj(o_fox.reshape(tp, FOX_WIDTH), o_gla.reshape(tp, GLA_WIDTH), w_out_b, y_p, g_ao,
                        mod_p, False, seq, _tile(seq, 512))

        proj_s, small_s = _in_proj(y_s, g_ap, mod_s, True, 1, w_big, w_small, b_small, ts)
        k_new = proj_s[:, COL_FK:COL_FV].reshape(db, FOX_KV_HEADS, FOX_HEAD_DIM)
        v_new = proj_s[:, COL_FV:COL_GQ].reshape(db, FOX_KV_HEADS, FOX_HEAD_DIM)
        lf_new = small_s[:, :FOX_HEADS]
        o_fox_s = _fox_decode(proj_s[:, :FOX_WIDTH].reshape(db, FOX_HEADS, FOX_HEAD_DIM),
                              jnp.repeat(k_new, FOX_GROUP, axis=1), jnp.repeat(v_new, FOX_GROUP, axis=1),
                              lf_new.reshape(db, FOX_HEADS, 1),
                              cache_k[l].reshape(n_pool, page * FOX_KV_HEADS, FOX_HEAD_DIM),
                              cache_v[l].reshape(n_pool, page * FOX_KV_HEADS, FOX_HEAD_DIM),
                              cache_logf[l].transpose(0, 2, 1), page_table)
        o_gla_s, s_new = _gla_decode(proj_s[:, COL_GQ:COL_GK].reshape(db, 1, GLA_QK_WIDTH),
                                     proj_s[:, COL_GK:COL_GV].reshape(db, 1, GLA_QK_WIDTH),
                                     proj_s[:, COL_GV:COL_GR].reshape(db, 1, GLA_WIDTH),
                                     proj_s[:, COL_GR:].reshape(db, 1, GLA_WIDTH),
                                     small_s.reshape(db, 1, SMALL_COLS), wup, bg, gn, state_gla[l])
        outs[4].append(k_new.reshape(db, 1, FOX_KV_HEADS, FOX_HEAD_DIM))
        outs[5].append(v_new.reshape(db, 1, FOX_KV_HEADS, FOX_HEAD_DIM))
        outs[6].append(lf_new.reshape(db, 1, FOX_HEADS))
        outs[7].append(s_new)
        y_s = _out_proj(o_fox_s.reshape(ts, FOX_WIDTH), o_gla_s.reshape(ts, GLA_WIDTH), w_out_b, y_s, g_ao,
                        mod_s, True, 1, ts)

        t_p, lg_p = _ffn_pre(y_p, g_fp, mod_p, False, seq, w_router[l], _tile(seq, 512))
        t_s, lg_s = _ffn_pre(y_s, g_fp, mod_s, True, 1, w_router[l], ts)
        t_all = jnp.concatenate([t_p, t_s], axis=0)
        idx, wts, rank, counts = _route(jnp.concatenate([lg_p, lg_s], axis=0), b_router[l])
        tm_e = EXPERT_ROW_BLOCK
        dest, slot_tok, slot_w, tables = _dispatch(idx, wts, rank, counts, tm_e)
        x_sorted = t_all.at[slot_tok].get(mode="promise_in_bounds")
        h_sorted = _expert_gate_up(tables, x_sorted, w_e_gate[l:l + 1], w_e_up[l:l + 1], tm_e, tf=512)
        y_sorted = _expert_down(tables, h_sorted, slot_w, w_e_down[l:l + 1], tm_e, tn=4096)
        gathered = y_sorted.at[dest.T].get(mode="promise_in_bounds")
        wsg, wsu, wsd = w_s_gate[l].astype(BF16), w_s_up[l].astype(BF16), w_s_down[l].astype(BF16)
        sh_p = _shared_mlp(t_p, wsg, wsu, wsd, _tile(seq, 512))
        sh_s = _shared_mlp(t_s, wsg, wsu, wsd, ts)
        tm_c = math.gcd(LANES, ts)
        y_p = _combine_final(gathered, 0, sh_p, y_p, g_fo, mod_p, False, seq, tm_c)
        y_s = _combine_final(gathered, tp // tm_c, sh_s, y_s, g_fo, mod_s, True, 1, tm_c)

    k_p, v_p, f_p, s_p, k_s, v_s, f_s, s_s = [jnp.stack(o) for o in outs]
    return (y_p.reshape(bsz, seq, d), y_s.reshape(db, 1, d), k_p, v_p, f_p, s_p, k_s, v_s, f_s, s_s)
```

```python
import functools
import math

import numpy as np
import jax
import jax.numpy as jnp
from jax import lax
from jax.experimental import pallas as pl
from jax.experimental.pallas import tpu as pltpu

F32 = jnp.float32
BF16 = jnp.bfloat16
HI = lax.Precision.HIGHEST
NT_DIMS = (((1,), (1,)), ((), ()))
TN_DIMS = (((0,), (0,)), ((), ()))

FOX_HEADS = 16
FOX_KV_HEADS = 4
FOX_GROUP = FOX_HEADS // FOX_KV_HEADS
FOX_HEAD_DIM = 128
FOX_WIDTH = FOX_HEADS * FOX_HEAD_DIM
FOX_KV_WIDTH = FOX_KV_HEADS * FOX_HEAD_DIM
GLA_HEADS = 4
GLA_DK = 256
GLA_DV = 512
GLA_QK_WIDTH = GLA_HEADS * GLA_DK
GLA_WIDTH = GLA_HEADS * GLA_DV
GLA_GATE_RANK = 16
GLA_GATE_TEMP = 16.0
GLA_CHUNK = 64
GLA_SUB = 16
N_EXPERTS = 64
TOP_K = 8
N_GROUPS = 8
TOPK_GROUPS = 4
EXPERTS_PER_GROUP = N_EXPERTS // N_GROUPS
ROUTED_SCALE = 2.5
NORM_EPS = 1e-6
EXPERT_ROW_BLOCK = 256
EXPERT_ROW_SUB = 128

LANES = 128
SUBLANES = 8
VMEM_LIMIT_BYTES = 56 * 1024 * 1024

COL_FQ = 0
COL_FK = COL_FQ + FOX_WIDTH
COL_FV = COL_FK + FOX_KV_WIDTH
COL_GQ = COL_FV + FOX_KV_WIDTH
COL_GK = COL_GQ + GLA_QK_WIDTH
COL_GV = COL_GK + GLA_QK_WIDTH
COL_GR = COL_GV + GLA_WIDTH
BIG_COLS = COL_GR + GLA_WIDTH
SMALL_COLS = LANES


def _params(*sem):
    return pltpu.CompilerParams(dimension_semantics=sem, vmem_limit_bytes=VMEM_LIMIT_BYTES)


def _sigmoid(x):
    return 1.0 / (1.0 + jnp.exp(-x))


def _silu(x):
    return x * _sigmoid(x)


def _log_sigmoid(x):
    return jnp.minimum(x, 0.0) - jnp.log(1.0 + jnp.exp(-jnp.abs(x)))


def _rms(x, g):
    return x * lax.rsqrt(jnp.mean(x * x, axis=-1, keepdims=True) + NORM_EPS) * g


def _iota(shape, dim):
    return lax.broadcasted_iota(jnp.int32, shape, dim)


def _mod_kernel(c_ref, w_ref, b_ref, o_ref):
    a = _silu(c_ref[...]).astype(BF16)
    o_ref[...] = jnp.dot(a, w_ref[...].astype(BF16), preferred_element_type=F32) + b_ref[...]


def _modulation(c_all, w_ada, b_ada, tn=512):
    mp, d = c_all.shape
    n = w_ada.shape[1]
    return pl.pallas_call(
        _mod_kernel,
        grid=(n // tn,),
        in_specs=[pl.BlockSpec((mp, d), lambda j: (0, 0)),
                  pl.BlockSpec((d, tn), lambda j: (0, j)),
                  pl.BlockSpec((1, tn), lambda j: (0, j))],
        out_specs=pl.BlockSpec((mp, tn), lambda j: (0, j)),
        out_shape=jax.ShapeDtypeStruct((mp, n), F32),
        compiler_params=_params("parallel"),
        name="modulation",
    )(c_all, w_ada, b_ada.reshape(1, n))


def _mod_spec(per_row, which, tm, d, tiles_per_seq):
    if per_row:
        return pl.BlockSpec((None, tm, d), lambda i, *_: (which, i, 0))
    return pl.BlockSpec((None, None, 1, d), lambda i, *_: (i // tiles_per_seq, which, 0, 0))


def _inproj_kernel(x_ref, g_ref, sh_ref, sc_ref, w_ref, ws_ref, bs_ref, o_ref, os_ref, ok_ref, ov_ref, h_ref):
    j = pl.program_id(1)
    tn = o_ref.shape[1]

    @pl.when(j == 0)
    def _():
        h = _rms(x_ref[...], g_ref[...]) * (1.0 + sc_ref[...]) + sh_ref[...]
        hb = h.astype(BF16)
        h_ref[...] = hb
        sm = jnp.dot(hb, ws_ref[...], preferred_element_type=F32) + bs_ref[...]
        os_ref[...] = jnp.where(_iota(sm.shape, 1) < FOX_HEADS, _log_sigmoid(sm), sm)

    y = jnp.dot(h_ref[...], w_ref[...], preferred_element_type=F32)
    o_ref[...] = y

    @pl.when(j == COL_FK // tn)
    def _():
        ok_ref[...] = y

    @pl.when(j == COL_FV // tn)
    def _():
        ov_ref[...] = y


def _in_proj(x, g, mod, per_row, rows_per_seq, w_big, w_small, b_small, tm):
    t, d = x.shape
    tn = FOX_KV_WIDTH
    tps = max(rows_per_seq // tm, 1)
    return pl.pallas_call(
        _inproj_kernel,
        grid=(t // tm, BIG_COLS // tn),
        in_specs=[pl.BlockSpec((tm, d), lambda i, j: (i, 0), pipeline_mode=pl.Buffered(1)),
                  pl.BlockSpec((1, d), lambda i, j: (0, 0)),
                  _mod_spec(per_row, 0, tm, d, tps),
                  _mod_spec(per_row, 1, tm, d, tps),
                  pl.BlockSpec((d, tn), lambda i, j: (0, j)),
                  pl.BlockSpec((d, SMALL_COLS), lambda i, j: (0, 0)),
                  pl.BlockSpec((1, SMALL_COLS), lambda i, j: (0, 0))],
        out_specs=[pl.BlockSpec((tm, tn), lambda i, j: (i, j)),
                   pl.BlockSpec((tm, SMALL_COLS), lambda i, j: (i, 0)),
                   pl.BlockSpec((tm, FOX_KV_WIDTH), lambda i, j: (i, 0)),
                   pl.BlockSpec((tm, FOX_KV_WIDTH), lambda i, j: (i, 0))],
        out_shape=[jax.ShapeDtypeStruct((t, BIG_COLS), F32),
                   jax.ShapeDtypeStruct((t, SMALL_COLS), F32),
                   jax.ShapeDtypeStruct((t, FOX_KV_WIDTH), F32),
                   jax.ShapeDtypeStruct((t, FOX_KV_WIDTH), F32)],
        scratch_shapes=[pltpu.VMEM((tm, d), BF16)],
        compiler_params=_params("parallel", "arbitrary"),
        name="in_proj",
    )(x, g, mod, mod, w_big, w_small, b_small)


def _cum_kernel(x_ref, o_ref, carry_ref):
    @pl.when(pl.program_id(1) == 0)
    def _():
        carry_ref[...] = jnp.zeros_like(carry_ref)

    x = x_ref[...]
    tl = x.shape[0]
    tri = (_iota((tl, tl), 1) <= _iota((tl, tl), 0)).astype(F32)
    cum = jnp.dot(tri, x, precision=HI, preferred_element_type=F32) + carry_ref[...]
    carry_ref[...] = cum[tl - 1:tl, :]
    o_ref[...] = cum.T[:FOX_HEADS, :]


def _cum_logf(small, tl):
    b, l, _ = small.shape
    return pl.pallas_call(
        _cum_kernel,
        grid=(b, l // tl),
        in_specs=[pl.BlockSpec((None, tl, SMALL_COLS), lambda i, t: (i, t, 0))],
        out_specs=pl.BlockSpec((None, FOX_HEADS, tl), lambda i, t: (i, 0, t)),
        out_shape=jax.ShapeDtypeStruct((b, FOX_HEADS, l), F32),
        scratch_shapes=[pltpu.VMEM((1, SMALL_COLS), F32)],
        compiler_params=_params("parallel", "arbitrary"),
        name="cum_logf",
    )(small)


def _fox_kernel(qi_tab, kj_tab, q_ref, k_ref, v_ref, ck_ref, o_ref, qs_ref, m_ref, l_ref, acc_ref, *, tq):
    p_id = pl.program_id(2)
    qi = qi_tab[p_id]
    kj = kj_tab[p_id]
    scale = FOX_HEAD_DIM ** -0.5

    @pl.when(kj == 0)
    def _():
        m_ref[...] = jnp.full_like(m_ref, -jnp.inf)
        l_ref[...] = jnp.zeros_like(l_ref)
        acc_ref[...] = jnp.zeros_like(acc_ref)
        for g in range(FOX_GROUP):
            qs_ref[g * tq:(g + 1) * tq, :] = (q_ref[:, g * FOX_HEAD_DIM:(g + 1) * FOX_HEAD_DIM] * scale).astype(BF16)

    def step(masked):
        kb = k_ref[...].astype(BF16)
        vb = v_ref[...].astype(BF16)
        s = lax.dot_general(qs_ref[...], kb, NT_DIMS, preferred_element_type=F32)
        ck = ck_ref[...]
        tk = kb.shape[0]
        parts = []
        for g in range(FOX_GROUP):
            sg = s[g * tq:(g + 1) * tq, :] - ck[g:g + 1, :]
            if masked:
                sg = jnp.where(_iota((tq, tk), 1) <= _iota((tq, tk), 0), sg, -jnp.inf)
            parts.append(sg)
        s = jnp.concatenate(parts, axis=0)
        m_prev = m_ref[...]
        m_next = jnp.maximum(m_prev, jnp.max(s, axis=1, keepdims=True))
        p = jnp.exp(s - m_next[:, :1])
        alpha = jnp.exp(m_prev - m_next)
        l_ref[...] = alpha * l_ref[...] + jnp.sum(p, axis=1, keepdims=True)
        m_ref[...] = m_next
        acc_ref[...] = alpha * acc_ref[...] + jnp.dot(p.astype(BF16), vb, preferred_element_type=F32)

    @pl.when(kj < qi)
    def _():
        step(False)

    @pl.when(kj == qi)
    def _():
        step(True)
        o = acc_ref[...] / l_ref[...]
        for g in range(FOX_GROUP):
            o_ref[:, g * FOX_HEAD_DIM:(g + 1) * FOX_HEAD_DIM] = o[g * tq:(g + 1) * tq, :].astype(o_ref.dtype)


def _fox_prompt(proj, cum4, tq):
    b, l, _ = proj.shape
    nq = l // tq
    pairs = [(i, j) for i in range(nq) for j in range(i + 1)]
    qi_tab = jnp.asarray(np.array([p[0] for p in pairs], np.int32))
    kj_tab = jnp.asarray(np.array([p[1] for p in pairs], np.int32))
    gw = FOX_GROUP * FOX_HEAD_DIM
    kcol = COL_FK // FOX_HEAD_DIM
    vcol = COL_FV // FOX_HEAD_DIM
    grid_spec = pltpu.PrefetchScalarGridSpec(
        num_scalar_prefetch=2,
        grid=(b, FOX_KV_HEADS, len(pairs)),
        in_specs=[pl.BlockSpec((None, tq, gw), lambda i, h, p, qt, kt: (i, qt[p], h)),
                  pl.BlockSpec((None, tq, FOX_HEAD_DIM), lambda i, h, p, qt, kt: (i, kt[p], kcol + h)),
                  pl.BlockSpec((None, tq, FOX_HEAD_DIM), lambda i, h, p, qt, kt: (i, kt[p], vcol + h)),
                  pl.BlockSpec((None, None, FOX_GROUP, tq), lambda i, h, p, qt, kt: (i, h, 0, kt[p]))],
        out_specs=pl.BlockSpec((None, tq, gw), lambda i, h, p, qt, kt: (i, qt[p], h)),
        scratch_shapes=[pltpu.VMEM((FOX_GROUP * tq, FOX_HEAD_DIM), BF16),
                        pltpu.VMEM((FOX_GROUP * tq, FOX_HEAD_DIM), F32),
                        pltpu.VMEM((FOX_GROUP * tq, FOX_HEAD_DIM), F32),
                        pltpu.VMEM((FOX_GROUP * tq, FOX_HEAD_DIM), F32)],
    )
    return pl.pallas_call(
        functools.partial(_fox_kernel, tq=tq),
        grid_spec=grid_spec,
        out_shape=jax.ShapeDtypeStruct((b, l, FOX_WIDTH), BF16),
        compiler_params=_params("parallel", "parallel", "arbitrary"),
        name="fox_prompt",
    )(qi_tab, kj_tab, proj, proj, proj, cum4)


def _foxdec_kernel(pt_ref, q_ref, kn_ref, vn_ref, lfn_ref, *refs, n_pages):
    k_refs = refs[:n_pages]
    v_refs = refs[n_pages:2 * n_pages]
    lf_refs = refs[2 * n_pages:3 * n_pages]
    o_ref = refs[3 * n_pages]
    del pt_ref
    scale = FOX_HEAD_DIM ** -0.5
    cols = k_refs[0].shape[0]
    page = cols // FOX_KV_HEADS
    q = q_ref[...] * scale
    qb = q.astype(BF16)
    valid = (_iota((FOX_HEADS, cols), 1) % FOX_KV_HEADS) == (_iota((FOX_HEADS, cols), 0) // FOX_GROUP)
    spread = (_iota((page, cols), 0) <= _iota((page, cols), 1) // FOX_KV_HEADS).astype(F32)

    lfs = [lf_refs[p][...] for p in range(n_pages)]
    cums = jnp.dot(jnp.concatenate(lfs, axis=0), spread, precision=HI, preferred_element_type=F32)
    off = jnp.zeros((FOX_HEADS, 1), F32)
    logits = []
    for p in range(n_pages):
        cum = cums[p * FOX_HEADS:(p + 1) * FOX_HEADS, :] + off
        off = off + jnp.sum(lfs[p], axis=1, keepdims=True)
        sp = lax.dot_general(qb, k_refs[p][...].astype(BF16), NT_DIMS, preferred_element_type=F32)
        logits.append(jnp.where(valid, sp - cum, -jnp.inf))
    s_new = jnp.sum(q * kn_ref[...], axis=1, keepdims=True) - (off + lfn_ref[...])

    m = s_new
    for lg in logits:
        m = jnp.maximum(m, jnp.max(lg, axis=1, keepdims=True))
    p_new = jnp.exp(s_new - m)
    l = p_new
    acc = p_new * vn_ref[...]
    for p in range(n_pages):
        pp = jnp.exp(logits[p] - m)
        l = l + jnp.sum(pp, axis=1, keepdims=True)
        acc = acc + jnp.dot(pp.astype(BF16), v_refs[p][...].astype(BF16), preferred_element_type=F32)
    o_ref[...] = (acc / l).astype(o_ref.dtype)


def _fox_decode(q, k_new, v_new, lf_new, cache_k, cache_v, cache_lft, page_table):
    db, n_pages = page_table.shape

    def page_spec(p, shape):
        return pl.BlockSpec((None,) + shape, lambda i, pt: (pt[i, p], 0, 0))

    row = pl.BlockSpec((None, FOX_HEADS, FOX_HEAD_DIM), lambda i, pt: (i, 0, 0))
    in_specs = [row, row, row, pl.BlockSpec((None, FOX_HEADS, 1), lambda i, pt: (i, 0, 0))]
    in_specs += [page_spec(p, cache_k.shape[1:]) for p in range(n_pages)]
    in_specs += [page_spec(p, cache_v.shape[1:]) for p in range(n_pages)]
    in_specs += [page_spec(p, cache_lft.shape[1:]) for p in range(n_pages)]
    grid_spec = pltpu.PrefetchScalarGridSpec(
        num_scalar_prefetch=1, grid=(db,), in_specs=in_specs,
        out_specs=pl.BlockSpec((None, FOX_HEADS, FOX_HEAD_DIM), lambda i, pt: (i, 0, 0)))
    return pl.pallas_call(
        functools.partial(_foxdec_kernel, n_pages=n_pages),
        grid_spec=grid_spec,
        out_shape=jax.ShapeDtypeStruct((db, FOX_HEADS, FOX_HEAD_DIM), BF16),
        compiler_params=_params("parallel"),
        name="fox_decode",
    )(page_table, q, k_new, v_new, lf_new, *([cache_k] * n_pages), *([cache_v] * n_pages),
      *([cache_lft] * n_pages))


def _gla_gate(gg, wup, bg):
    return _log_sigmoid(jnp.dot(gg, wup, precision=HI, preferred_element_type=F32) + bg) / GLA_GATE_TEMP


def _gla_out(o, g, gr):
    return _rms(o, g) * _silu(gr)


def _gla_head(q, k, v, r, la, gn, s_old):
    c = q.shape[0]
    tri = (_iota((c, c), 1) <= _iota((c, c), 0)).astype(F32)
    b = jnp.dot(tri, la, precision=HI, preferred_element_type=F32)
    b_last = b[c - 1:c, :]
    vb = v.astype(BF16)
    o = jnp.dot((q * jnp.exp(b)).astype(BF16), s_old.astype(BF16), preferred_element_type=F32)

    sb = min(GLA_SUB, c)
    rows_c = _iota((c, GLA_DK), 0)
    rows_s = _iota((sb, GLA_DK), 0)
    lane = _iota((sb, c), 1)
    a_blocks = []
    for i0 in range(0, c, sb):
        qi = q[i0:i0 + sb, :]
        bi = b[i0:i0 + sb, :]
        a = jnp.zeros((sb, c), F32)
        if i0 > 0:
            rr = b[i0:i0 + 1, :]
            qt = (qi * jnp.exp(bi - rr)).astype(BF16)
            kt = (k * jnp.exp(jnp.where(rows_c < i0, rr - b, -jnp.inf))).astype(BF16)
            a = lax.dot_general(qt, kt, NT_DIMS, preferred_element_type=F32)
        for jj in range(sb):
            j = i0 + jj
            w = jnp.exp(jnp.where(rows_s >= jj, bi - b[j:j + 1, :], -jnp.inf))
            col = jnp.sum(qi * w * k[j:j + 1, :], axis=1, keepdims=True)
            a = jnp.where(lane == j, col, a)
        a_blocks.append(a)
    a = jnp.concatenate(a_blocks, axis=0)
    o = o + jnp.dot(a.astype(BF16), vb, preferred_element_type=F32)

    kd = (k * jnp.exp(b_last - b)).astype(BF16)
    ones = jnp.ones((c, LANES), F32)
    decay = jnp.exp(lax.dot_general(la, ones, TN_DIMS, precision=HI, preferred_element_type=F32))
    decay = jnp.concatenate([decay] * (GLA_DV // LANES), axis=1)
    s_new = decay * s_old + lax.dot_general(kd, vb, TN_DIMS, preferred_element_type=F32)
    return _gla_out(o, gn, r), s_new


def _gla_kernel(q_ref, k_ref, *refs):
    v_refs = refs[:GLA_HEADS]
    r_refs = refs[GLA_HEADS:2 * GLA_HEADS]
    sm_ref, wup_ref, bg_ref, gn_ref, o_ref, s_ref = refs[2 * GLA_HEADS:]

    @pl.when(pl.program_id(1) == 0)
    def _():
        s_ref[...] = jnp.zeros_like(s_ref)

    gg = sm_ref[:, FOX_HEADS:FOX_HEADS + GLA_GATE_RANK]
    la = _gla_gate(gg, wup_ref[...], bg_ref[...])
    for h in range(GLA_HEADS):
        ks = slice(h * GLA_DK, (h + 1) * GLA_DK)
        o, s_new = _gla_head(q_ref[:, ks] * (GLA_DK ** -0.5), k_ref[:, ks], v_refs[h][...], r_refs[h][...],
                             la[:, ks], gn_ref[...], s_ref[h])
        o_ref[:, h * GLA_DV:(h + 1) * GLA_DV] = o.astype(o_ref.dtype)
        s_ref[h] = s_new


def _gla_prompt(proj, small, wup, bg, gn):
    b, l, _ = proj.shape
    c = GLA_CHUNK if l % GLA_CHUNK == 0 else l
    qc, kc = COL_GQ // GLA_QK_WIDTH, COL_GK // GLA_QK_WIDTH
    vc, rc = COL_GV // GLA_DV, COL_GR // GLA_DV

    def head_specs(col0):
        return [pl.BlockSpec((None, c, GLA_DV), functools.partial(lambda i, t, h: (i, t, col0 + h), h=h))
                for h in range(GLA_HEADS)]

    return pl.pallas_call(
        _gla_kernel,
        grid=(b, l // c),
        in_specs=[pl.BlockSpec((None, c, GLA_QK_WIDTH), lambda i, t: (i, t, qc)),
                  pl.BlockSpec((None, c, GLA_QK_WIDTH), lambda i, t: (i, t, kc))]
        + head_specs(vc) + head_specs(rc)
        + [pl.BlockSpec((None, c, SMALL_COLS), lambda i, t: (i, t, 0)),
           pl.BlockSpec((GLA_GATE_RANK, GLA_QK_WIDTH), lambda i, t: (0, 0)),
           pl.BlockSpec((1, GLA_QK_WIDTH), lambda i, t: (0, 0)),
           pl.BlockSpec((1, GLA_DV), lambda i, t: (0, 0))],
        out_specs=[pl.BlockSpec((None, c, GLA_WIDTH), lambda i, t: (i, t, 0)),
                   pl.BlockSpec((None, GLA_HEADS, GLA_DK, GLA_DV), lambda i, t: (i, 0, 0, 0))],
        out_shape=[jax.ShapeDtypeStruct((b, l, GLA_WIDTH), BF16),
                   jax.ShapeDtypeStruct((b, GLA_HEADS, GLA_DK, GLA_DV), F32)],
        compiler_params=_params("parallel", "arbitrary"),
        name="gla_prompt",
    )(proj, proj, *([proj] * (2 * GLA_HEADS)), small, wup, bg, gn)


def _gladec_kernel(q_ref, k_ref, v_ref, r_ref, sm_ref, wup_ref, bg_ref, gn_ref, s_ref, o_ref, so_ref):
    gg = sm_ref[:, FOX_HEADS:FOX_HEADS + GLA_GATE_RANK]
    la = _gla_gate(gg, wup_ref[...], bg_ref[...])
    rows = []
    for h in range(GLA_HEADS):
        ks = slice(h * GLA_DK, (h + 1) * GLA_DK)
        rows += [jnp.exp(la[:, ks]), k_ref[:, ks], q_ref[:, ks] * (GLA_DK ** -0.5)]
    n_rows = 2 * SUBLANES
    rows = jnp.concatenate(rows + [jnp.zeros((n_rows - len(rows), GLA_DK), F32)], axis=0)
    eye = (_iota((GLA_DK, GLA_DK), 0) == _iota((GLA_DK, GLA_DK), 1)).astype(BF16)
    cols = jnp.zeros((GLA_DK, n_rows), F32)
    rest = rows
    for _ in range(3):
        part = rest.astype(BF16)
        rest = rest - part.astype(F32)
        cols = cols + lax.dot_general(eye, part, NT_DIMS, preferred_element_type=F32)
    for h in range(GLA_HEADS):
        vs = slice(h * GLA_DV, (h + 1) * GLA_DV)
        decay, kc, qc = (cols[:, 3 * h + i:3 * h + i + 1] for i in range(3))
        s_new = decay * s_ref[h] + kc * v_ref[:, vs]
        so_ref[h] = s_new
        o = jnp.sum(qc * s_new, axis=0, keepdims=True)
        o_ref[:, vs] = _gla_out(o, gn_ref[...], r_ref[:, vs]).astype(o_ref.dtype)


def _gla_decode(gq, gk, gv, gr, small3, wup, bg, gn, state):
    db = gq.shape[0]
    st =pl.BlockSpec((None, GLA_HEADS, GLA_DK, GLA_DV), lambda i: (i, 0, 0, 0))
    return pl.pallas_call(
        _gladec_kernel,
        grid=(db,),
        in_specs=[pl.BlockSpec((None, 1, GLA_QK_WIDTH), lambda i: (i, 0, 0)),
                  pl.BlockSpec((None, 1, GLA_QK_WIDTH), lambda i: (i, 0, 0)),
                  pl.BlockSpec((None, 1, GLA_WIDTH), lambda i: (i, 0, 0)),
                  pl.BlockSpec((None, 1, GLA_WIDTH), lambda i: (i, 0, 0)),
                  pl.BlockSpec((None, 1, SMALL_COLS), lambda i: (i, 0, 0)),
                  pl.BlockSpec((GLA_GATE_RANK, GLA_QK_WIDTH), lambda i: (0, 0)),
                  pl.BlockSpec((1, GLA_QK_WIDTH), lambda i: (0, 0)),
                  pl.BlockSpec((1, GLA_DV), lambda i: (0, 0)),
                  st],
        out_specs=[pl.BlockSpec((None, 1, GLA_WIDTH), lambda i: (i, 0, 0)), st],
        out_shape=[jax.ShapeDtypeStruct((db, 1, GLA_WIDTH), BF16),
                   jax.ShapeDtypeStruct(state.shape, F32)],
        compiler_params=_params("parallel"),
        name="gla_decode",
    )(gq, gk, gv, gr, small3, wup, bg, gn, state)


def _outproj_kernel(a_ref, b_ref, wa_ref, wb_ref, x_ref, g_ref, gt_ref, o_ref, acc_ref):
    j = pl.program_id(1)
    nj = pl.num_programs(1)
    acc_ref[j] = (jnp.dot(a_ref[...], wa_ref[...], preferred_element_type=F32)
                  + jnp.dot(b_ref[...], wb_ref[...], preferred_element_type=F32))

    @pl.when(j == nj - 1)
    def _():
        n_chunks, _, tn = acc_ref.shape
        ss = jnp.zeros((acc_ref.shape[1], 1), F32)
        for c in range(n_chunks):
            z = acc_ref[c]
            ss = ss + jnp.sum(z * z, axis=1, keepdims=True)
        rs = lax.rsqrt(ss / (n_chunks * tn) + NORM_EPS)
        for c in range(n_chunks):
            cs = slice(c * tn, (c + 1) * tn)
            o_ref[:, cs] = x_ref[:, cs] + gt_ref[:, cs] * (acc_ref[c] * rs * g_ref[:, cs])


def _out_proj(o_fox, o_gla, w_out, x, g, mod, per_row, rows_per_seq, tm, tn=512):
    t, d = x.shape
    ka = o_fox.shape[1]
    tn = min(tn, d)
    tps = max(rows_per_seq // tm, 1)
    return pl.pallas_call(
        _outproj_kernel,
        grid=(t // tm, d // tn),
        in_specs=[pl.BlockSpec((tm, ka), lambda i, j: (i, 0)),
                  pl.BlockSpec((tm, ka), lambda i, j: (i, 0)),
                  pl.BlockSpec((ka, tn), lambda i, j: (0, j)),
                  pl.BlockSpec((ka, tn), lambda i, j: (1, j)),
                  pl.BlockSpec((tm, d), lambda i, j: (i, 0), pipeline_mode=pl.Buffered(1)),
                  pl.BlockSpec((1, d), lambda i, j: (0, 0)),
                  _mod_spec(per_row, 2, tm, d, tps)],
        out_specs=pl.BlockSpec((tm, d), lambda i, j: (i, 0)),
        out_shape=jax.ShapeDtypeStruct((t, d), F32),
        scratch_shapes=[pltpu.VMEM((d // tn, tm, tn), F32)],
        compiler_params=_params("parallel", "arbitrary"),
        name="out_proj",
    )(o_fox, o_gla, w_out, w_out, x, g, mod)


def _ffnpre_kernel(x_ref, g_ref, sh_ref, sc_ref, wr_ref, t_ref, lg_ref):
    h = _rms(x_ref[...], g_ref[...]) * (1.0 + sc_ref[...]) + sh_ref[...]
    t_ref[...] = h.astype(BF16)
    lg_ref[...] = jnp.dot(h, wr_ref[...], precision=HI, preferred_element_type=F32)


def _ffn_pre(x, g, mod, per_row, rows_per_seq, w_router, tm):
    t, d = x.shape
    tps = max(rows_per_seq // tm, 1)
    return pl.pallas_call(
        _ffnpre_kernel,
        grid=(t // tm,),
        in_specs=[pl.BlockSpec((tm, d), lambda i: (i, 0)),
                  pl.BlockSpec((1, d), lambda i: (0, 0)),
                  _mod_spec(per_row, 3, tm, d, tps),
                  _mod_spec(per_row, 4, tm, d, tps),
                  pl.BlockSpec((d, N_EXPERTS), lambda i: (0, 0))],
        out_specs=[pl.BlockSpec((tm, d), lambda i: (i, 0)),
                   pl.BlockSpec((tm, N_EXPERTS), lambda i: (i, 0))],
        out_shape=[jax.ShapeDtypeStruct((t, d), BF16),
                   jax.ShapeDtypeStruct((t, N_EXPERTS), F32)],
        compiler_params=_params("parallel"),
        name="ffn_pre",
    )(x, g, mod, mod, w_router)


def _new_expert(be_ref, i):
    return jnp.logical_or(i == 0, be_ref[i] != be_ref[jnp.maximum(i - 1, 0)])


def _stream_weights(be_ref, nu_ref, run_ref, rexp_ref, nr_ref, copies, cast):
    p = pl.program_id(0)
    i = pl.program_id(1)

    @pl.when(jnp.logical_and(i < nu_ref[0], _new_expert(be_ref, i)))
    def _():
        run = run_ref[i]
        last = run + 1 == nr_ref[0]

        @pl.when(jnp.logical_and(p == 0, i == 0))
        def _():
            for c in copies(p, be_ref[i]):
                c.start()

        for c in copies(p, be_ref[i]):
            c.wait()
        cast()

        @pl.when(jnp.logical_or(jnp.logical_not(last), p + 1 < pl.num_programs(0)))
        def _():
            for c in copies(jnp.where(last, p + 1, p), rexp_ref[jnp.where(last, 0, run + 1)]):
                c.start()


def _by_valid_rows(nv, out_ref, compute):
    tm = out_ref.shape[0]
    sub = min(EXPERT_ROW_SUB, tm)
    for n in range(sub, tm + 1, sub):
        @pl.when(jnp.logical_and(nv > n - sub, nv <= n))
        def _(n=n):
            out_ref[:n, :] = compute(n)
            if n < tm:
                out_ref[n:, :] = jnp.zeros((tm - n, out_ref.shape[1]), out_ref.dtype)

    @pl.when(nv <= 0)
    def _():
        out_ref[...] = jnp.zeros_like(out_ref)


def _gateup_kernel(be_ref, nu_ref, run_ref, rexp_ref, nr_ref, nv_ref, x_ref, wg_hbm, wu_hbm, h_ref,
                   wgf_ref, wuf_ref, wgb_ref, wub_ref, sem):
    tf = wgf_ref.shape[1]

    def copies(p, e):
        cols = pl.ds(pl.multiple_of(p * tf, tf), tf)
        return (pltpu.make_async_copy(wg_hbm.at[0, e, :, cols], wgf_ref, sem.at[0]),
                pltpu.make_async_copy(wu_hbm.at[0, e, :, cols], wuf_ref, sem.at[1]))

    def cast():
        wgb_ref[...] = wgf_ref[...].astype(BF16)
        wub_ref[...] = wuf_ref[...].astype(BF16)

    _stream_weights(be_ref, nu_ref, run_ref, rexp_ref, nr_ref, copies, cast)

    def compute(n):
        x = x_ref[:n, :]
        g = jnp.dot(x, wgb_ref[...], preferred_element_type=F32)
        u = jnp.dot(x, wub_ref[...], preferred_element_type=F32)
        return (_silu(g) * u).astype(h_ref.dtype)

    _by_valid_rows(nv_ref[pl.program_id(1)], h_ref, compute)


def _expert_gate_up(tables, x_sorted, w_gate, w_up, tm, tf):
    r, d = x_sorted.shape
    ff = w_gate.shape[-1]
    tf = min(tf, ff)
    grid_spec = pltpu.PrefetchScalarGridSpec(
        num_scalar_prefetch=len(tables), grid=(ff // tf, r // tm),
        in_specs=[pl.BlockSpec((tm, d), lambda p, i, *_: (i, 0)),
                  pl.BlockSpec(memory_space=pl.ANY), pl.BlockSpec(memory_space=pl.ANY)],
        out_specs=pl.BlockSpec((tm, tf), lambda p, i, *_: (i, p)),
        scratch_shapes=[pltpu.VMEM((d, tf), F32), pltpu.VMEM((d, tf), F32),
                        pltpu.VMEM((d, tf), BF16), pltpu.VMEM((d, tf), BF16),
                        pltpu.SemaphoreType.DMA((2,))])
    return pl.pallas_call(
        _gateup_kernel, grid_spec=grid_spec,
        out_shape=jax.ShapeDtypeStruct((r, ff), BF16),
        compiler_params=_params("arbitrary", "arbitrary"),
        name="expert_gate_up",
    )(*tables, x_sorted, w_gate, w_up)


def _down_kernel(be_ref, nu_ref, run_ref, rexp_ref, nr_ref, nv_ref, h_ref, sw_ref, wd_hbm, y_ref, wdf_ref, wdb_ref,
                 sem):
    tn = wdf_ref.shape[1]

    def copies(p, e):
        cols = pl.ds(pl.multiple_of(p * tn, tn), tn)
        return (pltpu.make_async_copy(wd_hbm.at[0, e, :, cols], wdf_ref, sem.at[0]),)

    def cast():
        wdb_ref[...] = wdf_ref[...].astype(BF16)

    _stream_weights(be_ref, nu_ref, run_ref, rexp_ref, nr_ref, copies, cast)

    def compute(n):
        y = jnp.dot(h_ref[:n, :], wdb_ref[...], preferred_element_type=F32)
        return (y * sw_ref[:n, :]).astype(y_ref.dtype)

    _by_valid_rows(nv_ref[pl.program_id(1)], y_ref, compute)


def _expert_down(tables, h_sorted, slot_w, w_down, tm, tn):
    r, ff = h_sorted.shape
    d = w_down.shape[-1]
    tn = min(tn, d)
    grid_spec = pltpu.PrefetchScalarGridSpec(
        num_scalar_prefetch=len(tables), grid=(d // tn, r // tm),
        in_specs=[pl.BlockSpec((tm, ff), lambda p, i, *_: (i, 0)),
                  pl.BlockSpec((tm, 1), lambda p, i, *_: (i, 0)),
                  pl.BlockSpec(memory_space=pl.ANY)],
        out_specs=pl.BlockSpec((tm, tn), lambda p, i, *_: (i, p)),
        scratch_shapes=[pltpu.VMEM((ff, tn), F32), pltpu.VMEM((ff, tn), BF16), pltpu.SemaphoreType.DMA((1,))])
    return pl.pallas_call(
        _down_kernel, grid_spec=grid_spec,
        out_shape=jax.ShapeDtypeStruct((r, d), BF16),
        compiler_params=_params("arbitrary", "arbitrary"),
        name="expert_down",
    )(*tables, h_sorted, slot_w, w_down)


def _shared_kernel(t_ref, wg_ref, wu_ref, wd_ref, o_ref):
    f = pl.program_id(1)
    t = t_ref[...]
    h = _silu(jnp.dot(t, wg_ref[...], preferred_element_type=F32)) * jnp.dot(t, wu_ref[...], preferred_element_type=F32)
    y = jnp.dot(h.astype(BF16), wd_ref[...], preferred_element_type=F32)

    @pl.when(f == 0)
    def _():
        o_ref[...] = y

    @pl.when(f > 0)
    def _():
        o_ref[...] += y


def _shared_mlp(t, wg, wu, wd, tm, tf=256):
    n, d = t.shape
    ff = wg.shape[1]
    return pl.pallas_call(
        _shared_kernel,
        grid=(n // tm, ff // tf),
        in_specs=[pl.BlockSpec((tm, d), lambda i, f: (i, 0)),
                  pl.BlockSpec((d, tf), lambda i, f: (0, f)),
                  pl.BlockSpec((d, tf), lambda i, f: (0, f)),
                  pl.BlockSpec((tf, d), lambda i, f: (f, 0))],
        out_specs=pl.BlockSpec((tm, d), lambda i, f: (i, 0)),
        out_shape=jax.ShapeDtypeStruct((n, d), F32),
        compiler_params=_params("parallel", "arbitrary"),
        name="shared_mlp",
    )(t, wg, wu, wd)


def _combine_kernel(y_ref, sh_ref, x_ref, g_ref, gt_ref, o_ref):
    z = sh_ref[...]
    for k in range(y_ref.shape[0]):
        z = z + y_ref[k].astype(F32)
    o_ref[...] = x_ref[...] + gt_ref[...] * _rms(z, g_ref[...])


def _combine_final(gathered, row_block0, shared, x, g, mod, per_row, rows_per_seq, tm):
    n, d = x.shape
    tps = max(rows_per_seq // tm, 1)
    return pl.pallas_call(
        _combine_kernel,
        grid=(n // tm,),
        in_specs=[pl.BlockSpec((TOP_K, tm, d), lambda i: (0, row_block0 + i, 0)),
                  pl.BlockSpec((tm, d), lambda i: (i, 0)),
                  pl.BlockSpec((tm, d), lambda i: (i, 0)),
                  pl.BlockSpec((1, d), lambda i: (0, 0)),
                  _mod_spec(per_row, 5, tm, d, tps)],
        out_specs=pl.BlockSpec((tm, d), lambda i: (i, 0)),
        out_shape=jax.ShapeDtypeStruct((n, d), F32),
        compiler_params=_params("parallel"),
        name="combine_final",
    )(gathered, shared, x, g, mod)


def _first_max(x, lane, n):
    m = jnp.max(x, axis=1, keepdims=True)
    return m, jnp.min(jnp.where(x == m, lane, n), axis=1, keepdims=True)


def _route_kernel(lg_ref, br_ref, idx_ref, w_ref, rank_ref, cnt_ref, carry_ref):
    @pl.when(pl.program_id(0) == 0)
    def _():
        carry_ref[...] = jnp.zeros_like(carry_ref)

    tm = lg_ref.shape[0]
    scores = _sigmoid(lg_ref[...])
    sel = scores + br_ref[...]
    lane = _iota((tm, N_EXPERTS), 1).astype(F32)
    grp = (_iota((tm, N_EXPERTS), 1) // EXPERTS_PER_GROUP).astype(F32)
    neg = -jnp.inf

    gsc = jnp.full((tm, N_EXPERTS), neg, F32)
    for g in range(N_GROUPS):
        mg = jnp.where(grp == g, sel, neg)
        m1, i1 = _first_max(mg, lane, N_EXPERTS)
        m2 = jnp.max(jnp.where(lane == i1, neg, mg), axis=1, keepdims=True)
        gsc = jnp.where(lane == g, m1 + m2, gsc)
    cand = jnp.full((tm, N_EXPERTS), neg, F32)
    for _ in range(TOPK_GROUPS):
        _, gi = _first_max(gsc, lane, N_EXPERTS)
        cand = jnp.where(grp == gi, sel, cand)
        gsc = jnp.where(lane == gi, neg, gsc)

    col = _iota((tm, TOP_K), 1)
    idx = jnp.zeros((tm, TOP_K), F32)
    wts = jnp.zeros((tm, TOP_K), F32)
    chosen = jnp.zeros((tm, N_EXPERTS), F32)
    hits = []
    for k in range(TOP_K):
        _, ik = _first_max(cand, lane, N_EXPERTS)
        hit = lane == ik
        hits.append(hit)
        cand = jnp.where(hit, neg, cand)
        chosen = jnp.where(hit, 1.0, chosen)
        idx = jnp.where(col == k, ik, idx)
        wts = jnp.where(col == k, jnp.sum(jnp.where(hit, scores, 0.0), axis=1, keepdims=True), wts)
    idx_ref[...] = idx.astype(jnp.int32)
    w_ref[...] = wts / jnp.sum(wts, axis=1, keepdims=True) * ROUTED_SCALE

    below = (_iota((tm, tm), 1) < _iota((tm, tm), 0)).astype(BF16)
    prefix = jnp.dot(below, chosen.astype(BF16), preferred_element_type=F32) + carry_ref[...]
    rank = jnp.zeros((tm, TOP_K), F32)
    for k in range(TOP_K):
        rank = jnp.where(col == k, jnp.sum(jnp.where(hits[k], prefix, 0.0), axis=1, keepdims=True), rank)
    rank_ref[...] = rank.astype(jnp.int32)
    carry_ref[...] += jnp.sum(chosen, axis=0, keepdims=True)
    cnt_ref[...] = carry_ref[...]


def _route(logits, b_router):
    t = logits.shape[0]
    tm = max(m for m in range(SUBLANES, 1025, SUBLANES) if t % m == 0)
    kspec = pl.BlockSpec((tm, TOP_K), lambda i: (i, 0))
    return pl.pallas_call(
        _route_kernel,
        grid=(t // tm,),
        in_specs=[pl.BlockSpec((tm, N_EXPERTS), lambda i: (i, 0)),
                  pl.BlockSpec((1, N_EXPERTS), lambda i: (0, 0))],
        out_specs=[kspec, kspec, kspec, pl.BlockSpec((1, N_EXPERTS), lambda i: (0, 0))],
        out_shape=[jax.ShapeDtypeStruct((t, TOP_K), jnp.int32),
                   jax.ShapeDtypeStruct((t, TOP_K), F32),
                   jax.ShapeDtypeStruct((t, TOP_K), jnp.int32),
                   jax.ShapeDtypeStruct((1, N_EXPERTS), F32)],
        scratch_shapes=[pltpu.VMEM((1, N_EXPERTS), F32)],
        compiler_params=_params("arbitrary"),
        name="route",
    )(logits, b_router.reshape(1, N_EXPERTS))


def _dispatch(idx, wts, rank, counts, tm):
    t = idx.shape[0]
    m = t * TOP_K
    counts = counts.reshape(N_EXPERTS).astype(jnp.int32)
    padded = (counts + tm - 1) // tm * tm
    pad_end = jnp.cumsum(padded)
    pad_start = pad_end - padded
    dest = pad_start[idx] + rank
    n_blocks = -(-(m + N_EXPERTS * (tm - 1)) // tm)
    pairs = jnp.stack([jnp.repeat(jnp.arange(t, dtype=jnp.int32), TOP_K),
                       lax.bitcast_convert_type(wts.reshape(-1), jnp.int32)], axis=-1)
    n_slots = n_blocks * tm
    init = jnp.stack([jnp.arange(n_slots, dtype=jnp.int32) % t, jnp.zeros((n_slots,), jnp.int32)], axis=-1)
    slots = init.at[dest.reshape(-1)].set(pairs, mode="promise_in_bounds", unique_indices=True)
    slot_tok = slots[:, 0]
    slot_w = lax.bitcast_convert_type(slots[:, 1], F32).reshape(-1, 1)
    starts = jnp.arange(n_blocks, dtype=jnp.int32) * tm
    block_expert = jnp.minimum(jnp.sum((pad_end[None, :] <= starts[:, None]).astype(jnp.int32), axis=1), N_EXPERTS - 1)
    n_used = (pad_end[-1] // tm).astype(jnp.int32).reshape(1)
    present = counts > 0
    run_of_expert = jnp.cumsum(present.astype(jnp.int32)) - 1
    run_expert = jnp.argsort(jnp.logical_not(present), stable=True).astype(jnp.int32)
    n_runs = jnp.sum(present.astype(jnp.int32)).reshape(1)
    n_valid = jnp.clip(counts[block_expert] - (starts - pad_start[block_expert]), 0, tm)
    n_valid = jnp.where(starts < pad_end[-1], n_valid, 0).astype(jnp.int32)
    tables = (block_expert, n_used, run_of_expert[block_expert], run_expert, n_runs, n_valid)
    return dest, slot_tok, slot_w, tables


def _tile(n, pref):
    return pref if n % pref == 0 else n


def kernel(x_prompt, x_sample, c_prompt, c_sample, cache_k, cache_v, cache_logf, state_gla, page_table, w_ada, b_ada, g_attn_pre, g_attn_post, g_ffn_pre, g_ffn_post, w_in, b_forget, w_gla_gate_up, b_gla_gate, g_gla_out, w_out, w_router, b_router, w_e_gate, w_e_up, w_e_down, w_s_gate, w_s_up, w_s_down):
    bsz, seq, d = x_prompt.shape
    db = x_sample.shape[0]
    depth = w_ada.shape[0]
    n_pool, page = cache_k.shape[1], cache_k.shape[2]
    tp, ts = bsz * seq, db
    y_p = x_prompt.reshape(tp, d)
    y_s = x_sample.reshape(ts, d)
    outs = [[] for _ in range(8)]

    n_mod = bsz + db
    mp = -(-n_mod // SUBLANES) * SUBLANES
    c_all = jnp.concatenate([c_prompt, c_sample, jnp.zeros((mp - n_mod, d), F32)], axis=0)

    o_fg = FOX_WIDTH + 2 * FOX_KV_WIDTH
    o_gq = o_fg + FOX_HEADS
    o_gg = o_gq + 2 * GLA_QK_WIDTH + GLA_WIDTH
    o_gr = o_gg + GLA_GATE_RANK

    for l in range(depth):
        mod = _modulation(c_all, w_ada[l], b_ada[l])
        mod_p = mod[:bsz].reshape(bsz, 6, 1, d)
        mod_s = mod[bsz:n_mod].reshape(db, 6, d).transpose(1, 0, 2)

        wl = w_in[l]
        w_big = jnp.concatenate([wl[:, :o_fg], wl[:, o_gq:o_gg], wl[:, o_gr:]], axis=1).astype(BF16)
        w_small = jnp.concatenate([wl[:, o_fg:o_gq], wl[:, o_gg:o_gr],
                                   jnp.zeros((d, SMALL_COLS - FOX_HEADS - GLA_GATE_RANK), F32)], axis=1).astype(BF16)
        b_small = jnp.concatenate([b_forget[l], jnp.zeros((SMALL_COLS - FOX_HEADS,), F32)]).reshape(1, SMALL_COLS)
        w_out_b = w_out[l].astype(BF16)
        wup = w_gla_gate_up[l]
        bg = b_gla_gate[l].reshape(1, GLA_QK_WIDTH)
        gn = g_gla_out[l].reshape(1, GLA_DV)
        g_ap = g_attn_pre[l].reshape(1, d)
        g_ao = g_attn_post[l].reshape(1, d)
        g_fp = g_ffn_pre[l].reshape(1, d)
        g_fo = g_ffn_post[l].reshape(1, d)

        tm_p = _tile(seq, 1024)
        proj, small, fk_p, fv_p = _in_proj(y_p, g_ap, mod_p, False, seq, w_big, w_small, b_small, tm_p)
        proj3 = proj.reshape(bsz, seq, BIG_COLS)
        small3 = small.reshape(bsz, seq, SMALL_COLS)
        cum = _cum_logf(small3, _tile(seq, 512))
        o_fox = _fox_prompt(proj3, cum.reshape(bsz, FOX_KV_HEADS, FOX_GROUP, seq), _tile(seq, 1024))
        o_gla, s_fin = _gla_prompt(proj3, small3, wup, bg, gn)
        outs[0].append(fk_p.reshape(bsz, seq, FOX_KV_HEADS, FOX_HEAD_DIM))
        outs[1].append(fv_p.reshape(bsz, seq, FOX_KV_HEADS, FOX_HEAD_DIM))
        outs[2].append(small3[:, :, :FOX_HEADS])
        outs[3].append(s_fin)
        y_p = _out_proj(o_fox.reshape(tp, FOX_WIDTH), o_gla.reshape(tp, GLA_WIDTH), w_out_b, y_p, g_ao,
                        mod_p, False, seq, _tile(seq, 512))

        proj_s, small_s, fk_s, fv_s = _in_proj(y_s, g_ap, mod_s, True, 1, w_big, w_small, b_small, ts)
        k_new = fk_s.reshape(db, FOX_KV_HEADS, FOX_HEAD_DIM)
        v_new = fv_s.reshape(db, FOX_KV_HEADS, FOX_HEAD_DIM)
        lf_new = small_s[:, :FOX_HEADS]
        o_fox_s = _fox_decode(proj_s[:, :FOX_WIDTH].reshape(db, FOX_HEADS, FOX_HEAD_DIM),
                              jnp.repeat(k_new, FOX_GROUP, axis=1), jnp.repeat(v_new, FOX_GROUP, axis=1),
                              lf_new.reshape(db, FOX_HEADS, 1),
                              cache_k[l].reshape(n_pool, page * FOX_KV_HEADS, FOX_HEAD_DIM),
                              cache_v[l].reshape(n_pool, page * FOX_KV_HEADS, FOX_HEAD_DIM),
                              cache_logf[l].transpose(0, 2, 1), page_table)
        o_gla_s, s_new = _gla_decode(proj_s[:, COL_GQ:COL_GK].reshape(db, 1, GLA_QK_WIDTH),
                                     proj_s[:, COL_GK:COL_GV].reshape(db, 1, GLA_QK_WIDTH),
                                     proj_s[:, COL_GV:COL_GR].reshape(db, 1, GLA_WIDTH),
                                     proj_s[:, COL_GR:].reshape(db, 1, GLA_WIDTH),
                                     small_s.reshape(db, 1, SMALL_COLS), wup, bg, gn, state_gla[l])
        outs[4].append(k_new.reshape(db, 1, FOX_KV_HEADS, FOX_HEAD_DIM))
        outs[5].append(v_new.reshape(db, 1, FOX_KV_HEADS, FOX_HEAD_DIM))
        outs[6].append(lf_new.reshape(db, 1, FOX_HEADS))
        outs[7].append(s_new)
        y_s = _out_proj(o_fox_s.reshape(ts, FOX_WIDTH), o_gla_s.reshape(ts, GLA_WIDTH), w_out_b, y_s, g_ao,
                        mod_s, True, 1, ts)

        t_p, lg_p = _ffn_pre(y_p, g_fp, mod_p, False, seq, w_router[l], _tile(seq, 512))
        t_s, lg_s = _ffn_pre(y_s, g_fp, mod_s, True, 1, w_router[l], ts)
        t_all = jnp.concatenate([t_p, t_s], axis=0)
        idx, wts, rank, counts = _route(jnp.concatenate([lg_p, lg_s], axis=0), b_router[l])
        tm_e = EXPERT_ROW_BLOCK
        dest, slot_tok, slot_w, tables = _dispatch(idx, wts, rank, counts, tm_e)
        x_sorted = t_all.at[slot_tok].get(mode="promise_in_bounds")
        h_sorted = _expert_gate_up(tables, x_sorted, w_e_gate[l:l + 1], w_e_up[l:l + 1], tm_e, tf=512)
        y_sorted = _expert_down(tables, h_sorted, slot_w, w_e_down[l:l + 1], tm_e, tn=4096)
        gathered = y_sorted.at[dest.T].get(mode="promise_in_bounds")
        wsg, wsu, wsd = w_s_gate[l].astype(BF16), w_s_up[l].astype(BF16), w_s_down[l].astype(BF16)
        sh_p = _shared_mlp(t_p, wsg, wsu, wsd, _tile(seq, 512))
        sh_s = _shared_mlp(t_s, wsg, wsu, wsd, ts)
        tm_c = math.gcd(LANES, ts)
        y_p = _combine_final(gathered, 0, sh_p, y_p, g_fo, mod_p, False, seq, tm_c)
        y_s = _combine_final(gathered, tp // tm_c, sh_s, y_s, g_fo, mod_s, True, 1, tm_c)

    k_p, v_p, f_p, s_p, k_s, v_s, f_s, s_s = [jnp.stack(o) for o in outs]
    return (y_p.reshape(bsz, seq, d), y_s.reshape(db, 1, d), k_p, v_p, f_p, s_p, k_s, v_s, f_s, s_s)
```
